```python
import math
import jax, jax.numpy as jnp
from jax import lax
import numpy as np

D_MODEL = 2048
BATCH = 2
SEQ = 16384
DEPTH = 2
DEC_BATCH = 8
DEC_SEQ = 64
PAST_LEN = 4096

CHUNK = 64
N_META = 16
N_A = DEPTH // 2
N_B = DEPTH - N_A
D_FF = 5632
CONV_W = 31
N_HEADS = 32
HEAD_DIM = 64
KV_HEADS = 4
GROUP = N_HEADS // KV_HEADS
WINDOW = 128
W_CHUNKS = -(-WINDOW // CHUNK)
WIN_ROWS = W_CHUNKS * CHUNK
BAND = (W_CHUNKS + 1) * CHUNK
N_BUCKETS = 32
MAX_DISTANCE = 128
EPS = 1e-6
SCALE = HEAD_DIM ** -0.5
NEG_INF = -1e30

kernel_name = "yoco_conformer_swa_sink_stream_step"


def rms_norm(x, g):
    xf = x.astype(jnp.float32)
    y = xf * lax.rsqrt(jnp.mean(xf * xf, axis=-1, keepdims=True) + EPS)
    return (y * g.astype(jnp.float32)).astype(x.dtype)


def layer_norm(x, g, b):
    xf = x.astype(jnp.float32)
    xc = xf - jnp.mean(xf, axis=-1, keepdims=True)
    y = xc * lax.rsqrt(jnp.mean(xc * xc, axis=-1, keepdims=True) + EPS)
    return (y * g.astype(jnp.float32) + b.astype(jnp.float32)).astype(x.dtype)


def half_ffn(x, g, w_gate, w_up, w_down):
    h = rms_norm(x, g)
    return 0.5 * ((jax.nn.silu(h @ w_gate) * (h @ w_up)) @ w_down)


def conv_module(x, buf, g, w_pw1, b_pw1, w_dw, b_dw, ln_g, ln_b, w_pw2, b_pw2):
    h = rms_norm(x, g)
    a, gate = jnp.split(h @ w_pw1 + b_pw1, 2, axis=-1)
    u = a * jax.nn.sigmoid(gate)
    up = jnp.concatenate([buf.astype(u.dtype), u], axis=1)
    y = lax.conv_general_dilated(
        up, w_dw.astype(u.dtype)[:, None, :], window_strides=(1,), padding="VALID",
        dimension_numbers=("NWC", "WIO", "NWC"), feature_group_count=D_MODEL) + b_dw
    y = jax.nn.silu(layer_norm(y, ln_g, ln_b))
    return y @ w_pw2 + b_pw2, up[:, up.shape[1] - (CONV_W - 1):]


def shared_kv(x, kv_norm, w_k, w_v, k_norm):
    n, t = x.shape[:2]
    h = rms_norm(x, kv_norm)
    k = rms_norm((h @ w_k).reshape(n, t, KV_HEADS, HEAD_DIM), k_norm)
    v = (h @ w_v).reshape(n, t, KV_HEADS, HEAD_DIM)
    return k, v


def t5_bias(table, rel):
    nb = N_BUCKETS // 2
    max_exact = nb // 2
    n = jnp.abs(rel)
    large = max_exact + (jnp.log(jnp.maximum(n, 1).astype(jnp.float32) / max_exact)
                         / math.log(MAX_DISTANCE / max_exact) * (nb - max_exact)).astype(jnp.int32)
    large = jnp.minimum(large, nb - 1)
    bucket = jnp.where(rel > 0, nb, 0) + jnp.where(n < max_exact, n, large)
    b = jnp.moveaxis(table[bucket].astype(jnp.float32), -1, -3)
    return b.reshape(b.shape[:-3] + (KV_HEADS, GROUP) + b.shape[-2:])


def sink_softmax(logits_list, sink):
    lead = logits_list[0].shape[:-1]
    col = jnp.broadcast_to(sink.astype(jnp.float32).reshape(KV_HEADS, GROUP, 1, 1), lead + (1,))
    p = jax.nn.softmax(jnp.concatenate(logits_list + [col], axis=-1), axis=-1)
    bounds = [int(b) for b in np.cumsum([l.shape[-1] for l in logits_list])]
    return jnp.split(p, bounds, axis=-1)[:-1]


def window_attention_prompt(q, k, v, k_meta, v_meta, sink, table):
    n, t = q.shape[:2]
    nc = t // CHUNK
    qg = q.reshape(n, nc, CHUNK, KV_HEADS, GROUP, HEAD_DIM)

    def band(a):
        pad = jnp.zeros((n, W_CHUNKS * CHUNK) + a.shape[2:], a.dtype)
        ap = jnp.concatenate([pad, a], axis=1).reshape((n, nc + W_CHUNKS, CHUNK) + a.shape[2:])
        return jnp.concatenate([ap[:, j:j + nc] for j in range(W_CHUNKS + 1)], axis=2)

    kb, vb = band(k), band(v)
    i = jnp.arange(CHUNK)
    s = jnp.arange(BAND)
    c = jnp.arange(nc)
    m = jnp.arange(N_META)
    bias_band = t5_bias(table, s[None, :] - W_CHUNKS * CHUNK - i[:, None])
    q_pos = N_META + c[:, None] * CHUNK + i[None, :]
    bias_meta = t5_bias(table, m[None, None, :] - q_pos[:, :, None])
    valid = (c[:, None] - W_CHUNKS + s[None, :] // CHUNK >= 0)[:, None, None, None, :]
    lm = jnp.einsum("ncqkgd,nskd->nckgqs", qg, k_meta).astype(jnp.float32) + bias_meta
    lb = jnp.einsum("ncqkgd,ncskd->nckgqs", qg, kb).astype(jnp.float32) + bias_band
    lb = jnp.where(valid, lb, NEG_INF)
    pm, pb = sink_softmax([lm, lb], sink)
    o = (jnp.einsum("nckgqs,nskd->ncqkgd", pm.astype(v.dtype), v_meta)
         + jnp.einsum("nckgqs,ncskd->ncqkgd", pb.astype(v.dtype), vb))
    return o.reshape(n, t, N_HEADS * HEAD_DIM)


def window_attention_sample(q, k, v, cache_meta_k, cache_meta_v, cache_win_k, cache_win_v, sink, table):
    n, s_new = q.shape[:2]
    win = cache_win_k.shape[1]
    qg = q.reshape(n, s_new, KV_HEADS, GROUP, HEAD_DIM)
    kc = jnp.concatenate([cache_meta_k.astype(k.dtype), cache_win_k.astype(k.dtype), k], axis=1)
    vc = jnp.concatenate([cache_meta_v.astype(v.dtype), cache_win_v.astype(v.dtype), v], axis=1)
    q_pos = N_META + PAST_LEN + jnp.arange(s_new)
    k_pos = jnp.concatenate([jnp.arange(N_META), N_META + PAST_LEN - win + jnp.arange(win), q_pos])
    bias = t5_bias(table, k_pos[None, :] - q_pos[:, None])
    logits = jnp.einsum("nqkgd,nskd->nkgqs", qg, kc).astype(jnp.float32) + bias
    (p,) = sink_softmax([logits], sink)
    o = jnp.einsum("nkgqs,nskd->nqkgd", p.astype(vc.dtype), vc)
    return o.reshape(n, s_new, N_HEADS * HEAD_DIM)


def trunk(x, conv_bufs, n_lead, attend, ffn_p, conv_p, kv_p, attn_p):
    ffn_norm, ffn_w_gate, ffn_w_up, ffn_w_down = ffn_p
    conv_norm, w_pw1, b_pw1, w_dw, b_dw, ln_g, ln_b, w_pw2, b_pw2 = conv_p
    kv_norm, w_k, w_v, k_norm = kv_p
    attn_norm, w_q, q_norm, w_o = attn_p
    new_bufs = []
    k = v = k_lead = v_lead = None
    for i in range(DEPTH):
        if i == N_A:
            k_all, v_all = shared_kv(x, kv_norm, w_k, w_v, k_norm)
            k_lead, v_lead = k_all[:, :n_lead], v_all[:, :n_lead]
            k, v, x = k_all[:, n_lead:], v_all[:, n_lead:], x[:, n_lead:]
        x = x + half_ffn(x, ffn_norm[i, 0], ffn_w_gate[i, 0], ffn_w_up[i, 0], ffn_w_down[i, 0])
        if i < N_A:
            y, nb = conv_module(x, conv_bufs[i], conv_norm[i], w_pw1[i], b_pw1[i], w_dw[i], b_dw[i],
                                ln_g[i], ln_b[i], w_pw2[i], b_pw2[i])
            new_bufs.append(nb)
        else:
            l = i - N_A
            h = rms_norm(x, attn_norm[l])
            q = rms_norm((h @ w_q[l]).reshape(x.shape[:2] + (N_HEADS, HEAD_DIM)), q_norm[l]) * SCALE
            y = attend(q, k, v, k_lead, v_lead, l) @ w_o[l]
        x = x + y
        x = x + half_ffn(x, ffn_norm[i, 1], ffn_w_gate[i, 1], ffn_w_up[i, 1], ffn_w_down[i, 1])
    return x, jnp.stack(new_bufs), k, v, k_lead, v_lead


def setup_inputs(seed: int = 0) -> dict:
    key = jax.random.key(seed)
    ks = jax.random.split(key, 40)

    def nrm(k, shape, scale):
        return jax.random.normal(k, shape, jnp.float32) * scale

    win_rows = min(WIN_ROWS, PAST_LEN)
    return {
        "x_prompt": nrm(ks[0], (BATCH, SEQ, D_MODEL), 1.0),
        "x_sample": nrm(ks[1], (DEC_BATCH, DEC_SEQ, D_MODEL), 1.0),
        "state_conv": nrm(ks[2], (N_A, DEC_BATCH, CONV_W - 1, D_MODEL), 0.5),
        "cache_meta_k": nrm(ks[3], (DEC_BATCH, N_META, KV_HEADS, HEAD_DIM), 1.0),
        "cache_meta_v": nrm(ks[4], (DEC_BATCH, N_META, KV_HEADS, HEAD_DIM), 1.0),
        "cache_win_k": nrm(ks[5], (DEC_BATCH, win_rows, KV_HEADS, HEAD_DIM), 1.0),
        "cache_win_v": nrm(ks[6], (DEC_BATCH, win_rows, KV_HEADS, HEAD_DIM), 1.0),
        "meta_tokens": nrm(ks[7], (N_META, D_MODEL), 1.0),
        "ffn_norm": 1.0 + nrm(ks[8], (DEPTH, 2, D_MODEL), 0.05),
        "ffn_w_gate": nrm(ks[9], (DEPTH, 2, D_MODEL, D_FF), D_MODEL ** -0.5),
        "ffn_w_up": nrm(ks[10], (DEPTH, 2, D_MODEL, D_FF), D_MODEL ** -0.5),
        "ffn_w_down": nrm(ks[11], (DEPTH, 2, D_FF, D_MODEL), D_FF ** -0.5),
        "conv_norm": 1.0 + nrm(ks[12], (N_A, D_MODEL), 0.05),
        "conv_w_pw1": nrm(ks[13], (N_A, D_MODEL, 2 * D_MODEL), D_MODEL ** -0.5),
        "conv_b_pw1": nrm(ks[14], (N_A, 2 * D_MODEL), 0.02),
        "conv_w_dw": nrm(ks[15], (N_A, CONV_W, D_MODEL), CONV_W ** -0.5),
        "conv_b_dw": nrm(ks[16], (N_A, D_MODEL), 0.02),
        "conv_ln_g": 1.0 + nrm(ks[17], (N_A, D_MODEL), 0.05),
        "conv_ln_b": nrm(ks[18], (N_A, D_MODEL), 0.02),
        "conv_w_pw2": nrm(ks[19], (N_A, D_MODEL, D_MODEL), D_MODEL ** -0.5),
        "conv_b_pw2": nrm(ks[20], (N_A, D_MODEL), 0.02),
        "kv_norm": 1.0 + nrm(ks[21], (D_MODEL,), 0.05),
        "w_k": nrm(ks[22], (D_MODEL, KV_HEADS * HEAD_DIM), D_MODEL ** -0.5),
        "w_v": nrm(ks[23], (D_MODEL, KV_HEADS * HEAD_DIM), D_MODEL ** -0.5),
        "k_norm": 1.0 + nrm(ks[24], (HEAD_DIM,), 0.05),
        "attn_norm": 1.0 + nrm(ks[25], (N_B, D_MODEL), 0.05),
        "w_q": nrm(ks[26], (N_B, D_MODEL, N_HEADS * HEAD_DIM), D_MODEL ** -0.5),
        "q_norm": 1.0 + nrm(ks[27], (N_B, HEAD_DIM), 0.05),
        "sinks": nrm(ks[28], (N_B, N_HEADS), 0.5),
        "w_o": nrm(ks[29], (N_B, N_HEADS * HEAD_DIM, D_MODEL), (N_HEADS * HEAD_DIM) ** -0.5),
        "rel_bias_table": nrm(ks[30], (N_BUCKETS, N_HEADS), 0.5),
    }


def reference(x_prompt, x_sample, state_conv, cache_meta_k, cache_meta_v, cache_win_k, cache_win_v,
              meta_tokens, ffn_norm, ffn_w_gate, ffn_w_up, ffn_w_down,
              conv_norm, conv_w_pw1, conv_b_pw1, conv_w_dw, conv_b_dw, conv_ln_g, conv_ln_b,
              conv_w_pw2, conv_b_pw2, kv_norm, w_k, w_v, k_norm,
              attn_norm, w_q, q_norm, sinks, w_o, rel_bias_table):
    ffn_p = (ffn_norm, ffn_w_gate, ffn_w_up, ffn_w_down)
    conv_p = (conv_norm, conv_w_pw1, conv_b_pw1, conv_w_dw, conv_b_dw, conv_ln_g, conv_ln_b,
              conv_w_pw2, conv_b_pw2)
    kv_p = (kv_norm, w_k, w_v, k_norm)
    attn_p = (attn_norm, w_q, q_norm, w_o)

    n_p = x_prompt.shape[0]
    meta = jnp.broadcast_to(meta_tokens.astype(x_prompt.dtype)[None], (n_p, N_META, D_MODEL))
    x0 = jnp.concatenate([meta, x_prompt], axis=1)
    zero_bufs = jnp.zeros((N_A, n_p, CONV_W - 1, D_MODEL), x_prompt.dtype)

    def attend_prompt(q, k, v, k_lead, v_lead, l):
        return window_attention_prompt(q, k, v, k_lead, v_lead, sinks[l], rel_bias_table)

    y_prompt, p_conv, p_k, p_v, p_meta_k, p_meta_v = trunk(
        x0, zero_bufs, N_META, attend_prompt, ffn_p, conv_p, kv_p, attn_p)
    p_win_k = p_k[:, p_k.shape[1] - WIN_ROWS:]
    p_win_v = p_v[:, p_v.shape[1] - WIN_ROWS:]

    def attend_sample(q, k, v, k_lead, v_lead, l):
        return window_attention_sample(q, k, v, cache_meta_k, cache_meta_v, cache_win_k, cache_win_v,
                                       sinks[l], rel_bias_table)

    y_sample, s_conv, s_k, s_v, _, _ = trunk(
        x_sample, state_conv, 0, attend_sample, ffn_p, conv_p, kv_p, attn_p)

    return (y_prompt, y_sample, p_conv, p_meta_k, p_meta_v, p_win_k, p_win_v, s_conv, s_k, s_v)
```

```python
import functools
import math

import jax
import jax.numpy as jnp
from jax import lax
from jax.experimental import pallas as pl
from jax.experimental.pallas import tpu as pltpu

D_MODEL = 2048
D_FF = 5632
CHUNK = 64
N_META = 16
CONV_W = 31
N_HEADS = 32
HEAD_DIM = 64
KV_HEADS = 4
GROUP = N_HEADS // KV_HEADS
KV_DIM = KV_HEADS * HEAD_DIM
WINDOW = 128
W_CHUNKS = -(-WINDOW // CHUNK)
WIN_ROWS = W_CHUNKS * CHUNK
BAND = (W_CHUNKS + 1) * CHUNK
N_KEYS = BAND + N_META
N_BUCKETS = 32
MAX_DISTANCE = 128
PAST_LEN = 4096
EPS = 1e-6
SCALE = HEAD_DIM ** -0.5
NEG_INF = -1e30

HALO = 32
HALO_PAD = HALO - (CONV_W - 1)
V7X_VMEM_LIMIT = 56 * 1024 * 1024

F32 = jnp.float32
BF16 = jnp.bfloat16


def _params(n_axes, vmem=V7X_VMEM_LIMIT):
    return pltpu.CompilerParams(
        dimension_semantics=("arbitrary",) * n_axes, vmem_limit_bytes=vmem)


def _rms_bf16(x, g):
    ms = jnp.mean(x * x, axis=-1, keepdims=True)
    return (x * lax.rsqrt(ms + EPS) * g).astype(BF16)


def _ffn_body(x_ref, g_ref, wg_ref, wu_ref, wd_ref, o_ref, h_ref):
    @pl.when(pl.program_id(1) == 0)
    def _():
        x = x_ref[...]
        h_ref[...] = _rms_bf16(x, g_ref[...])
        o_ref[...] = x

    h = h_ref[...]
    gate = jnp.dot(h, wg_ref[...], preferred_element_type=F32)
    up = jnp.dot(h, wu_ref[...], preferred_element_type=F32)
    act = (0.5 * gate * jax.nn.sigmoid(gate) * up).astype(BF16)
    o_ref[...] += jnp.dot(act, wd_ref[...], preferred_element_type=F32)


def _ffn(x, g, wg, wu, wd, *, tm, tf=512):
    rows = x.shape[0]
    return pl.pallas_call(
        _ffn_body,
        grid=(rows // tm, D_FF // tf),
        in_specs=[
            pl.BlockSpec((tm, D_MODEL), lambda i, j: (i, 0)),
            pl.BlockSpec((1, D_MODEL), lambda i, j: (0, 0)),
            pl.BlockSpec((D_MODEL, tf), lambda i, j: (0, j)),
            pl.BlockSpec((D_MODEL, tf), lambda i, j: (0, j)),
            pl.BlockSpec((tf, D_MODEL), lambda i, j: (j, 0)),
        ],
        out_specs=pl.BlockSpec((tm, D_MODEL), lambda i, j: (i, 0)),
        out_shape=jax.ShapeDtypeStruct((rows, D_MODEL), F32),
        scratch_shapes=[pltpu.VMEM((tm, D_MODEL), BF16)],
        compiler_params=_params(2),
        name="ffn",
    )(x, g.reshape(1, D_MODEL), wg, wu, wd)


def _pw1_body(x_ref, g_ref, wa_ref, wg_ref, ba_ref, bg_ref, u_ref, h_ref):
    @pl.when(pl.program_id(1) == 0)
    def _():
        h_ref[...] = _rms_bf16(x_ref[...], g_ref[...])

    h = h_ref[...]
    a = jnp.dot(h, wa_ref[...], preferred_element_type=F32) + ba_ref[...]
    gate = jnp.dot(h, wg_ref[...], preferred_element_type=F32) + bg_ref[...]
    u_ref[...] = a * jax.nn.sigmoid(gate)


def _pw1(x, g, w, b, *, tm, tn=512):
    rows = x.shape[0]
    nj = D_MODEL // tn
    b2 = b.reshape(1, 2 * D_MODEL)
    return pl.pallas_call(
        _pw1_body,
        grid=(rows // tm, nj),
        in_specs=[
            pl.BlockSpec((tm, D_MODEL), lambda i, j: (i, 0)),
            pl.BlockSpec((1, D_MODEL), lambda i, j: (0, 0)),
            pl.BlockSpec((D_MODEL, tn), lambda i, j: (0, j)),
            pl.BlockSpec((D_MODEL, tn), lambda i, j: (0, j + nj)),
            pl.BlockSpec((1, tn), lambda i, j: (0, j)),
            pl.BlockSpec((1, tn), lambda i, j: (0, j + nj)),
        ],
        out_specs=pl.BlockSpec((tm, tn), lambda i, j: (i, j)),
        out_shape=jax.ShapeDtypeStruct((rows, D_MODEL), F32),
        scratch_shapes=[pltpu.VMEM((tm, D_MODEL), BF16)],
        compiler_params=_params(2),
        name="pw1_glu",
    )(x, g.reshape(1, D_MODEL), w, w, b2, b2)


def _dwconv_body(u_ref, halo_ref, hist_ref, w_ref, b_ref, lg_ref, lb_ref, z_ref,
                 win_ref, y_ref, *, tm, rb, lb):
    first = pl.program_id(1) == 0
    win_ref[0:HALO, :] = jnp.where(first, hist_ref[0], halo_ref[0])
    win_ref[HALO:HALO + tm, :] = u_ref[0]
    for r0 in range(0, tm, rb):
        for l0 in range(0, D_MODEL, lb):
            acc = jnp.zeros((rb, lb), F32)
            for k in range(CONV_W):
                s = r0 + HALO_PAD + k
                acc = acc + w_ref[k:k + 1, l0:l0 + lb] * win_ref[s:s + rb, l0:l0 + lb]
            y_ref[r0:r0 + rb, l0:l0 + lb] = acc + b_ref[:, l0:l0 + lb]
    y = y_ref[...]
    mu = jnp.mean(y, axis=-1, keepdims=True)
    yc = y - mu
    var = jnp.mean(yc * yc, axis=-1, keepdims=True)
    ln = yc * lax.rsqrt(var + EPS) * lg_ref[...] + lb_ref[...]
    z_ref[0] = (ln * jax.nn.sigmoid(ln)).astype(BF16)


def _dwconv(u, hist, w_dw, b_dw, ln_g, ln_b, *, tm):
    n, t, _ = u.shape
    per_stream = hist.shape[0] == n
    hb = tm // HALO
    body = functools.partial(_dwconv_body, tm=tm, rb=min(tm, 32), lb=256)
    row = lambda a: a.reshape(1, D_MODEL)
    return pl.pallas_call(
        body,
        grid=(n, t // tm),
        in_specs=[
            pl.BlockSpec((1, tm, D_MODEL), lambda i, j: (i, j, 0)),
            pl.BlockSpec((1, HALO, D_MODEL), lambda i, j: (i, jnp.maximum(j * hb - 1, 0), 0)),
            pl.BlockSpec((1, HALO, D_MODEL), (lambda i, j: (i, 0, 0)) if per_stream
                         else (lambda i, j: (0, 0, 0))),
            pl.BlockSpec((CONV_W, D_MODEL), lambda i, j: (0, 0)),
            pl.BlockSpec((1, D_MODEL), lambda i, j: (0, 0)),
            pl.BlockSpec((1, D_MODEL), lambda i, j: (0, 0)),
            pl.BlockSpec((1, D_MODEL), lambda i, j: (0, 0)),
        ],
        out_specs=pl.BlockSpec((1, tm, D_MODEL), lambda i, j: (i, j, 0)),
        out_shape=jax.ShapeDtypeStruct((n, t, D_MODEL), BF16),
        scratch_shapes=[pltpu.VMEM((tm + HALO, D_MODEL), F32), pltpu.VMEM((tm, D_MODEL), F32)],
        compiler_params=_params(2),
        name="dwconv_ln_silu",
    )(u, u, hist, w_dw, row(b_dw), row(ln_g), row(ln_b))


def _mm_res_body(z_ref, w_ref, b_ref, x_ref, o_ref):
    o_ref[...] = (x_ref[...] + jnp.dot(z_ref[...], w_ref[...], preferred_element_type=F32)
                  + b_ref[...])


def _mm_res_nobias_body(z_ref, w_ref, x_ref, o_ref):
    o_ref[...] = x_ref[...] + jnp.dot(z_ref[...], w_ref[...], preferred_element_type=F32)


def _mm_res(z, w, b, x, *, tm, tn=512):
    rows = x.shape[0]
    z_spec = pl.BlockSpec((tm, D_MODEL), lambda i, j: (i, 0))
    w_spec = pl.BlockSpec((D_MODEL, tn), lambda i, j: (0, j))
    x_spec = pl.BlockSpec((tm, tn), lambda i, j: (i, j))
    if b is None:
        body, in_specs, args = _mm_res_nobias_body, [z_spec, w_spec, x_spec], (z, w, x)
    else:
        body = _mm_res_body
        in_specs = [z_spec, w_spec, pl.BlockSpec((1, tn), lambda i, j: (0, j)), x_spec]
        args = (z, w, b.reshape(1, D_MODEL), x)
    return pl.pallas_call(
        body,
        grid=(rows // tm, D_MODEL // tn),
        in_specs=in_specs,
        out_specs=x_spec,
        out_shape=jax.ShapeDtypeStruct((rows, D_MODEL), F32),
        compiler_params=_params(2),
        name="matmul_residual",
    )(*args)


def _kv_body(x_ref, g_ref, w_ref, kn_ref, k_ref, v_ref):
    h = _rms_bf16(x_ref[...], g_ref[...])
    kv = jnp.dot(h, w_ref[...], preferred_element_type=F32)
    k = kv[:, :KV_DIM]
    lane_head = lax.broadcasted_iota(jnp.int32, (1, KV_DIM), 1) // HEAD_DIM
    inv = jnp.zeros_like(k)
    for j in range(KV_HEADS):
        sel = lane_head == j
        ss = jnp.sum(jnp.where(sel, k * k, 0.0), axis=-1, keepdims=True)
        inv = jnp.where(sel, lax.rsqrt(ss * (1.0 / HEAD_DIM) + EPS), inv)
    k_ref[...] = k * inv * kn_ref[...]
    v_ref[...] = kv[:, KV_DIM:]


def _kv(x, g, w_kv, k_norm, *, tm):
    rows = x.shape[0]
    out = jax.ShapeDtypeStruct((rows, KV_DIM), F32)
    return pl.pallas_call(
        _kv_body,
        grid=(rows // tm,),
        in_specs=[
            pl.BlockSpec((tm, D_MODEL), lambda i: (i, 0)),
            pl.BlockSpec((1, D_MODEL), lambda i: (0, 0)),
            pl.BlockSpec((D_MODEL, 2 * KV_DIM), lambda i: (0, 0)),
            pl.BlockSpec((1, KV_DIM), lambda i: (0, 0)),
        ],
        out_specs=[pl.BlockSpec((tm, KV_DIM), lambda i: (i, 0))] * 2,
        out_shape=[out, out],
        compiler_params=_params(1),
        name="shared_kv",
    )(x, g.reshape(1, D_MODEL), w_kv, jnp.tile(k_norm, KV_HEADS).reshape(1, KV_DIM))


def _q_body(x_ref, g_ref, w_ref, q_ref, h_ref):
    @pl.when(pl.program_id(1) == 0)
    def _():
        h_ref[...] = _rms_bf16(x_ref[...], g_ref[...])

    q_ref[...] = jnp.dot(h_ref[...], w_ref[...], preferred_element_type=F32).astype(BF16)


def _q_proj(x, g, w, *, tm, tn=512):
    rows = x.shape[0]
    return pl.pallas_call(
        _q_body,
        grid=(rows // tm, D_MODEL // tn),
        in_specs=[
            pl.BlockSpec((tm, D_MODEL), lambda i, j: (i, 0)),
            pl.BlockSpec((1, D_MODEL), lambda i, j: (0, 0)),
            pl.BlockSpec((D_MODEL, tn), lambda i, j: (0, j)),
        ],
        out_specs=pl.BlockSpec((tm, tn), lambda i, j: (i, j)),
        out_shape=jax.ShapeDtypeStruct((rows, D_MODEL), BF16),
        scratch_shapes=[pltpu.VMEM((tm, D_MODEL), BF16)],
        compiler_params=_params(2),
        name="q_proj",
    )(x, g.reshape(1, D_MODEL), w)


def _attn_body(q_ref, k0_ref, k1_ref, k2_ref, km_ref, v0_ref, v1_ref, v2_ref, vm_ref,
               bias_ref, sink_ref, wq_ref, o_ref):
    kc = jnp.concatenate([k0_ref[0], k1_ref[0], k2_ref[0], km_ref[0]], axis=0).astype(BF16)
    vc = jnp.concatenate([v0_ref[0], v1_ref[0], v2_ref[0], vm_ref[0]], axis=0).astype(BF16)
    q = q_ref[0].astype(F32)
    lane_head = lax.broadcasted_iota(jnp.int32, (1, KV_DIM), 1) // HEAD_DIM
    blocks = []
    for g in range(GROUP):
        qg = q[:, g * KV_DIM:(g + 1) * KV_DIM]
        for j in range(KV_HEADS):
            blocks.append(jnp.where(lane_head == j, qg, 0.0))
    qm = jnp.concatenate(blocks, axis=0)
    ss = jnp.sum(qm * qm, axis=-1, keepdims=True)
    inv = lax.rsqrt(ss * (1.0 / HEAD_DIM) + EPS) * SCALE
    qw = (qm * wq_ref[...]).astype(BF16)
    logits = lax.dot_general(qw, kc, (((1,), (1,)), ((), ())), preferred_element_type=F32)
    logits = logits * inv + bias_ref[0]
    sink = sink_ref[...]
    m = jnp.maximum(jnp.max(logits, axis=-1, keepdims=True), sink)
    p = jnp.exp(logits - m)
    denom = jnp.sum(p, axis=-1, keepdims=True) + jnp.exp(sink - m)
    o = jnp.dot((p / denom).astype(BF16), vc, preferred_element_type=F32)
    for g in range(GROUP):
        og = jnp.zeros((CHUNK, KV_DIM), F32)
        for j in range(KV_HEADS):
            r0 = (g * KV_HEADS + j) * CHUNK
            og = jnp.where(lane_head == j, o[r0:r0 + CHUNK], og)
        o_ref[0, :, g * KV_DIM:(g + 1) * KV_DIM] = og.astype(BF16)


def _attention(q, k, v, k_meta, v_meta, bias, sink_col, wq_row, *, chunk_offset):
    n, t, _ = q.shape
    off = chunk_offset

    def band_spec(back):
        return pl.BlockSpec((1, CHUNK, KV_DIM),
                            lambda i, c: (i, jnp.maximum(c + off - back, 0), 0))

    meta_spec = pl.BlockSpec((1, N_META, KV_DIM), lambda i, c: (i, 0, 0))
    rows = N_HEADS * CHUNK
    return pl.pallas_call(
        _attn_body,
        grid=(n, t // CHUNK),
        in_specs=[
            pl.BlockSpec((1, CHUNK, D_MODEL), lambda i, c: (i, c, 0)),
            band_spec(2), band_spec(1), band_spec(0), meta_spec,
            band_spec(2), band_spec(1), band_spec(0), meta_spec,
            pl.BlockSpec((1, rows, N_KEYS), lambda i, c: (jnp.minimum(c + off, W_CHUNKS), 0, 0)),
            pl.BlockSpec((rows, 1), lambda i, c: (0, 0)),
            pl.BlockSpec((1, KV_DIM), lambda i, c: (0, 0)),
        ],
        out_specs=pl.BlockSpec((1, CHUNK, D_MODEL), lambda i, c: (i, c, 0)),
        out_shape=jax.ShapeDtypeStruct((n, t, D_MODEL), BF16),
        compiler_params=_params(2),
        name="window_attention",
    )(q, k, k, k, k_meta, v, v, v, v_meta, bias, sink_col, wq_row)


def _t5_bucket(rel):
    nb = N_BUCKETS // 2
    max_exact = nb // 2
    n = jnp.abs(rel)
    large = max_exact + (jnp.log(jnp.maximum(n, 1).astype(jnp.float32) / max_exact)
                         / math.log(MAX_DISTANCE / max_exact) * (nb - max_exact)).astype(jnp.int32)
    large = jnp.minimum(large, nb - 1)
    return jnp.where(rel > 0, nb, 0) + jnp.where(n < max_exact, n, large)


def _stack_heads(b):
    s = b.shape[1]
    b = jnp.moveaxis(b.astype(F32), -1, 0).reshape(KV_HEADS, GROUP, CHUNK, s)
    return jnp.transpose(b, (1, 0, 2, 3)).reshape(N_HEADS * CHUNK, s)


def _bias_tables(table):
    i = jnp.arange(CHUNK)
    s = jnp.arange(BAND)
    m = jnp.arange(N_META)
    band = _stack_heads(table[_t5_bucket(s[None, :] - W_CHUNKS * CHUNK - i[:, None])])
    out = []
    for c in range(W_CHUNKS + 1):
        q_pos = N_META + c * CHUNK + i
        meta = _stack_heads(table[_t5_bucket(m[None, :] - q_pos[:, None])])
        valid = (c - W_CHUNKS + s // CHUNK >= 0)[None, :]
        out.append(jnp.concatenate([jnp.where(valid, band, NEG_INF), meta], axis=1))
    return jnp.stack(out)


def _perm_heads_cols(w):
    r = w.shape[0]
    return jnp.transpose(w.reshape(r, KV_HEADS, GROUP, HEAD_DIM), (0, 2, 1, 3)).reshape(r, D_MODEL)


def kernel(x_prompt, x_sample, state_conv, cache_meta_k, cache_meta_v, cache_win_k, cache_win_v,
           meta_tokens, ffn_norm, ffn_w_gate, ffn_w_up, ffn_w_down, conv_norm, conv_w_pw1,
           conv_b_pw1, conv_w_dw, conv_b_dw, conv_ln_g, conv_ln_b, conv_w_pw2, conv_b_pw2,
           kv_norm, w_k, w_v, k_norm, attn_norm, w_q, q_norm, sinks, w_o, rel_bias_table):
    n_p, seq, _ = x_prompt.shape
    n_s, s_new, _ = x_sample.shape
    rows_s = n_s * s_new
    pad_rows = CHUNK - N_META
    tm_big = 512
    tm_small = rows_s + N_META + pad_rows
    n_small = tm_small // s_new

    wg, wu, wd = (w.astype(BF16) for w in (ffn_w_gate, ffn_w_up, ffn_w_down))
    w_pw1 = conv_w_pw1[0].astype(BF16)
    w_pw2 = conv_w_pw2[0].astype(BF16)
    w_kv = jnp.concatenate([w_k, w_v], axis=1).astype(BF16)
    w_qp = _perm_heads_cols(w_q[0]).astype(BF16)
    w_op = _perm_heads_cols(w_o[0].T).T.astype(BF16)

    xb = x_prompt.reshape(n_p * seq, D_MODEL)
    xs = jnp.concatenate([x_sample.reshape(rows_s, D_MODEL), meta_tokens.astype(F32),
                          jnp.zeros((pad_rows, D_MODEL), F32)], axis=0)

    def ffn(x, layer, half, tm):
        return _ffn(x, ffn_norm[layer, half], wg[layer, half], wu[layer, half], wd[layer, half],
                    tm=tm)

    xs = ffn(xs, 0, 0, tm_small)
    xb = ffn(xb, 0, 0, tm_big)
    us = _pw1(xs, conv_norm[0], w_pw1, conv_b_pw1[0], tm=tm_small)
    ub = _pw1(xb, conv_norm[0], w_pw1, conv_b_pw1[0], tm=tm_big)

    us3 = us.reshape(n_small, s_new, D_MODEL)
    ub3 = ub.reshape(n_p, seq, D_MODEL)
    zpad = jnp.zeros((n_s, HALO_PAD, D_MODEL), F32)
    hist_s = jnp.concatenate([
        jnp.concatenate([zpad, state_conv[0].astype(F32)], axis=1),
        jnp.zeros((n_small - n_s, HALO, D_MODEL), F32)], axis=0)
    u_meta = us[rows_s:rows_s + N_META]
    hist_b = jnp.concatenate([jnp.zeros((HALO - N_META, D_MODEL), F32), u_meta], axis=0)[None]

    conv_args = (conv_w_dw[0], conv_b_dw[0], conv_ln_g[0], conv_ln_b[0])
    zs = _dwconv(us3, hist_s, *conv_args, tm=s_new).reshape(tm_small, D_MODEL)
    zb = _dwconv(ub3, hist_b, *conv_args, tm=256).reshape(n_p * seq, D_MODEL)
    xs = _mm_res(zs, w_pw2, conv_b_pw2[0], xs, tm=tm_small)
    xb = _mm_res(zb, w_pw2, conv_b_pw2[0], xb, tm=tm_big)
    xs = ffn(xs, 0, 1, tm_small)
    xb = ffn(xb, 0, 1, tm_big)

    ks, vs = _kv(xs, kv_norm, w_kv, k_norm, tm=tm_small)
    kb, vb = _kv(xb, kv_norm, w_kv, k_norm, tm=tm_big)
    k_new, v_new = ks[:rows_s], vs[:rows_s]
    k_lead, v_lead = ks[rows_s:rows_s + N_META], vs[rows_s:rows_s + N_META]
    xs = xs[:rows_s]

    xs = ffn(xs, 1, 0, rows_s)
    xb = ffn(xb, 1, 0, tm_big)
    qs = _q_proj(xs, attn_norm[0], w_qp, tm=rows_s).reshape(n_s, s_new, D_MODEL)
    qb = _q_proj(xb, attn_norm[0], w_qp, tm=tm_big).reshape(n_p, seq, D_MODEL)

    bias = _bias_tables(rel_bias_table)
    head_of_block = (jnp.arange(KV_HEADS)[None, :] * GROUP + jnp.arange(GROUP)[:, None]).reshape(-1)
    sink_col = jnp.repeat(sinks[0].astype(F32)[head_of_block], CHUNK).reshape(N_HEADS * CHUNK, 1)
    wq_row = jnp.tile(q_norm[0].astype(F32), KV_HEADS).reshape(1, KV_DIM)

    flat_kv = lambda a: a.reshape(a.shape[0], a.shape[1], KV_DIM).astype(F32)
    k_cat = jnp.concatenate([flat_kv(cache_win_k), k_new.reshape(n_s, s_new, KV_DIM)], axis=1)
    v_cat = jnp.concatenate([flat_kv(cache_win_v), v_new.reshape(n_s, s_new, KV_DIM)], axis=1)
    os_ = _attention(qs, k_cat, v_cat, flat_kv(cache_meta_k), flat_kv(cache_meta_v),
                     bias, sink_col, wq_row, chunk_offset=W_CHUNKS)
    lead = lambda a: jnp.broadcast_to(a[None], (n_p, N_META, KV_DIM))
    ob = _attention(qb, kb.reshape(n_p, seq, KV_DIM), vb.reshape(n_p, seq, KV_DIM),
                    lead(k_lead), lead(v_lead), bias, sink_col, wq_row, chunk_offset=0)

    xs = _mm_res(os_.reshape(rows_s, D_MODEL), w_op, None, xs, tm=rows_s)
    xb = _mm_res(ob.reshape(n_p * seq, D_MODEL), w_op, None, xb, tm=tm_big)
    xs = ffn(xs, 1, 1, rows_s)
    xb = ffn(xb, 1, 1, tm_big)

    heads = lambda a, n: a.reshape(n, -1, KV_HEADS, HEAD_DIM)
    keep = CONV_W - 1
    return (
        xb.reshape(n_p, seq, D_MODEL),
        xs.reshape(n_s, s_new, D_MODEL),
        ub3[:, seq - keep:][None],
        heads(lead(k_lead), n_p),
        heads(lead(v_lead), n_p),
        heads(kb.reshape(n_p, seq, KV_DIM)[:, seq - WIN_ROWS:], n_p),
        heads(vb.reshape(n_p, seq, KV_DIM)[:, seq - WIN_ROWS:], n_p),
        us3[:n_s, s_new - keep:][None],
        heads(k_new, n_s),
        heads(v_new, n_s),
    )
```

```python
import functools
import math

import jax
import jax.numpy as jnp
from jax import lax
from jax.experimental import pallas as pl
from jax.experimental.pallas import tpu as pltpu

D_MODEL = 2048
D_FF = 5632
CHUNK = 64
N_META = 16
CONV_W = 31
N_HEADS = 32
HEAD_DIM = 64
KV_HEADS = 4
GROUP = N_HEADS // KV_HEADS
KV_DIM = KV_HEADS * HEAD_DIM
WINDOW = 128
W_CHUNKS = -(-WINDOW // CHUNK)
WIN_ROWS = W_CHUNKS * CHUNK
BAND = (W_CHUNKS + 1) * CHUNK
N_BUCKETS = 32
MAX_DISTANCE = 128
EPS = 1e-6
SCALE = HEAD_DIM ** -0.5
NEG_INF = -1e30

LANES = 128
SUBLANES = 8
N_SLABS = D_MODEL // LANES
HALO = 32
HALO_PAD = HALO - (CONV_W - 1)
SEG_PAD = 4
KEY_COLS = 256
KEY_PAD = KEY_COLS - BAND - N_META
V7X_VMEM_LIMIT = 56 * 1024 * 1024

F32 = jnp.float32
BF16 = jnp.bfloat16


def _params(n_axes, vmem=V7X_VMEM_LIMIT):
    return pltpu.CompilerParams(
        dimension_semantics=("arbitrary",) * n_axes, vmem_limit_bytes=vmem)


def _resident(shape):
    return pl.BlockSpec(shape, lambda *_: (0,) * len(shape), pipeline_mode=pl.Buffered(1))


def _rms_bf16(x, g):
    ms = jnp.mean(x * x, axis=-1, keepdims=True)
    return (x * lax.rsqrt(ms + EPS) * g).astype(BF16)


def _ffn_body(x_ref, g_ref, wg_ref, wu_ref, wd_ref, o_ref, h_ref):
    @pl.when(pl.program_id(1) == 0)
    def _():
        x = x_ref[...]
        h_ref[...] = _rms_bf16(x, g_ref[...])
        o_ref[...] = x

    h = h_ref[...]
    gate = jnp.dot(h, wg_ref[...], preferred_element_type=F32)
    up = jnp.dot(h, wu_ref[...], preferred_element_type=F32)
    act = (0.5 * gate * jax.nn.sigmoid(gate) * up).astype(BF16)
    o_ref[...] += jnp.dot(act, wd_ref[...], preferred_element_type=F32)


def _ffn(x, g, wg, wu, wd, layer, half, *, tm, tf=512):
    rows = x.shape[0]
    return pl.pallas_call(
        _ffn_body,
        grid=(rows // tm, D_FF // tf),
        in_specs=[
            pl.BlockSpec((tm, D_MODEL), lambda i, j: (i, 0)),
            pl.BlockSpec((None, 1, D_MODEL), lambda i, j: (layer * 2 + half, 0, 0)),
            pl.BlockSpec((None, None, D_MODEL, tf), lambda i, j: (layer, half, 0, j)),
            pl.BlockSpec((None, None, D_MODEL, tf), lambda i, j: (layer, half, 0, j)),
            pl.BlockSpec((None, None, tf, D_MODEL), lambda i, j: (layer, half, j, 0)),
        ],
        out_specs=pl.BlockSpec((tm, D_MODEL), lambda i, j: (i, 0)),
        out_shape=jax.ShapeDtypeStruct((rows, D_MODEL), F32),
        scratch_shapes=[pltpu.VMEM((tm, D_MODEL), BF16)],
        compiler_params=_params(2),
        name="ffn",
    )(x, g.reshape(-1, 1, D_MODEL), wg, wu, wd)


def _pw1_body(x_ref, g_ref, w_ref, b_ref, u_ref, *, tn):
    h = _rms_bf16(x_ref[...], g_ref[...])
    for c0 in range(0, D_MODEL, tn):
        a = jnp.dot(h, w_ref[:, c0:c0 + tn], preferred_element_type=F32) + b_ref[:, c0:c0 + tn]
        gate = (jnp.dot(h, w_ref[:, D_MODEL + c0:D_MODEL + c0 + tn], preferred_element_type=F32)
                + b_ref[:, D_MODEL + c0:D_MODEL + c0 + tn])
        u_ref[:, c0:c0 + tn] = a * jax.nn.sigmoid(gate)


def _pw1(x, g, w, b, *, tm, tn=512):
    rows = x.shape[0]
    return pl.pallas_call(
        functools.partial(_pw1_body, tn=tn),
        grid=(rows // tm,),
        in_specs=[
            pl.BlockSpec((tm, D_MODEL), lambda i: (i, 0)),
            _resident((1, D_MODEL)),
            _resident((D_MODEL, 2 * D_MODEL)),
            _resident((1, 2 * D_MODEL)),
        ],
        out_specs=pl.BlockSpec((tm, D_MODEL), lambda i: (i, 0)),
        out_shape=jax.ShapeDtypeStruct((rows, D_MODEL), F32),
        compiler_params=_params(1),
        name="pw1_glu",
    )(x, g.reshape(1, D_MODEL), w, b.reshape(1, 2 * D_MODEL))


def _dwconv_body(u_ref, halo_ref, hist_ref, w_ref, b_ref, lg_ref, lb_ref, z_ref,
                 win_ref, y_ref, *, tm):
    seg = tm // SUBLANES
    pitch = seg + SEG_PAD
    n_seg = (HALO + tm) // seg
    q_group = min(seg, SUBLANES)

    halo = jnp.where(pl.program_id(1) == 0, hist_ref[0], halo_ref[0])
    for c in range(N_SLABS):
        lanes = slice(c * LANES, (c + 1) * LANES)
        for sg in range(n_seg):
            r0 = sg * seg
            src = halo[r0:r0 + seg, lanes] if r0 < HALO else u_ref[0, r0 - HALO:r0 - HALO + seg, lanes]
            win_ref[c, sg * pitch:sg * pitch + seg, :] = src

    def slab(c, carry):
        for q0 in range(0, seg, q_group):
            accs = [b_ref[c]] * q_group
            for s in range(HALO_PAD, HALO + 1):
                wv = w_ref[c, s - HALO_PAD]
                for qi in range(q_group):
                    rho = q0 + qi + s
                    start = rho + SEG_PAD * (rho // seg)
                    accs[qi] = accs[qi] + wv * win_ref[c, pl.ds(start, SUBLANES, stride=pitch), :]
            for qi in range(q_group):
                y_ref[c, pl.ds(q0 + qi, SUBLANES, stride=seg), :] = accs[qi]
        return carry

    lax.fori_loop(0, N_SLABS, slab, 0)

    tot = y_ref[0]
    for c in range(1, N_SLABS):
        tot = tot + y_ref[c]
    mu = jnp.sum(tot, axis=-1, keepdims=True) * (1.0 / D_MODEL)
    sq = jnp.zeros((tm, LANES), F32)
    for c in range(N_SLABS):
        d = y_ref[c] - mu
        sq = sq + d * d
    inv = lax.rsqrt(jnp.sum(sq, axis=-1, keepdims=True) * (1.0 / D_MODEL) + EPS)
    for c in range(N_SLABS):
        lanes = slice(c * LANES, (c + 1) * LANES)
        ln = (y_ref[c] - mu) * inv * lg_ref[:, lanes] + lb_ref[:, lanes]
        z_ref[0, :, lanes] = (ln * jax.nn.sigmoid(ln)).astype(BF16)


def _dwconv(u, hist, w_dw, b_dw, ln_g, ln_b, *, tm):
    n, t, _ = u.shape
    per_stream = hist.shape[0] == n
    hb = tm // HALO
    seg = tm // SUBLANES
    win_rows = -(-((HALO + tm) // seg * (seg + SEG_PAD)) // SUBLANES) * SUBLANES
    row = lambda a: a.reshape(1, D_MODEL)
    w_b = jnp.broadcast_to(
        jnp.transpose(w_dw.astype(F32).reshape(CONV_W, N_SLABS, LANES), (1, 0, 2))[:, :, None, :],
        (N_SLABS, CONV_W, SUBLANES, LANES))
    b_b = jnp.broadcast_to(b_dw.astype(F32).reshape(N_SLABS, 1, LANES), (N_SLABS, SUBLANES, LANES))
    return pl.pallas_call(
        functools.partial(_dwconv_body, tm=tm),
        grid=(n, t // tm),
        in_specs=[
            pl.BlockSpec((1, tm, D_MODEL), lambda i, j: (i, j, 0)),
            pl.BlockSpec((1, HALO, D_MODEL), lambda i, j: (i, jnp.maximum(j * hb - 1, 0), 0)),
            pl.BlockSpec((1, HALO, D_MODEL), (lambda i, j: (i, 0, 0)) if per_stream
                         else (lambda i, j: (0, 0, 0))),
            _resident((N_SLABS, CONV_W, SUBLANES, LANES)),
            _resident((N_SLABS, SUBLANES, LANES)),
            _resident((1, D_MODEL)),
            _resident((1, D_MODEL)),
        ],
        out_specs=pl.BlockSpec((1, tm, D_MODEL), lambda i, j: (i, j, 0)),
        out_shape=jax.ShapeDtypeStruct((n, t, D_MODEL), BF16),
        scratch_shapes=[pltpu.VMEM((N_SLABS, win_rows, LANES), F32),
                        pltpu.VMEM((N_SLABS, tm, LANES), F32)],
        compiler_params=_params(2),
        name="dwconv_ln_silu",
    )(u, u, hist, w_b, b_b, row(ln_g), row(ln_b))


def _mm_res_body(z_ref, w_ref, b_ref, x_ref, o_ref, *, tn):
    z = z_ref[...]
    for c0 in range(0, D_MODEL, tn):
        cols = slice(c0, c0 + tn)
        o_ref[:, cols] = (x_ref[:, cols] + jnp.dot(z, w_ref[:, cols], preferred_element_type=F32)
                          + b_ref[:, cols])


def _mm_res_nobias_body(z_ref, w_ref, x_ref, o_ref, *, tn):
    z = z_ref[...]
    for c0 in range(0, D_MODEL, tn):
        cols = slice(c0, c0 + tn)
        o_ref[:, cols] = x_ref[:, cols] + jnp.dot(z, w_ref[:, cols], preferred_element_type=F32)


def _mm_res(z, w, b, x, *, tm, tn=512):
    rows = x.shape[0]
    row_spec = pl.BlockSpec((tm, D_MODEL), lambda i: (i, 0))
    w_spec = _resident((D_MODEL, D_MODEL))
    if b is None:
        body, in_specs, args = _mm_res_nobias_body, [row_spec, w_spec, row_spec], (z, w, x)
    else:
        body = _mm_res_body
        in_specs = [row_spec, w_spec, _resident((1, D_MODEL)), row_spec]
        args = (z, w, b.reshape(1, D_MODEL), x)
    return pl.pallas_call(
        functools.partial(body, tn=tn),
        grid=(rows // tm,),
        in_specs=in_specs,
        out_specs=row_spec,
        out_shape=jax.ShapeDtypeStruct((rows, D_MODEL), F32),
        compiler_params=_params(1),
        name="matmul_residual",
    )(*args)


def _kv_body(x_ref, g_ref, w_ref, kn_ref, k_ref, v_ref):
    h = _rms_bf16(x_ref[...], g_ref[...])
    kv = jnp.dot(h, w_ref[...], preferred_element_type=F32)
    k = kv[:, :KV_DIM]
    lane_head = lax.broadcasted_iota(jnp.int32, (1, KV_DIM), 1) // HEAD_DIM
    inv = jnp.zeros_like(k)
    for j in range(KV_HEADS):
        sel = lane_head == j
        ss = jnp.sum(jnp.where(sel, k * k, 0.0), axis=-1, keepdims=True)
        inv = jnp.where(sel, lax.rsqrt(ss * (1.0 / HEAD_DIM) + EPS), inv)
    k_ref[...] = k * inv * kn_ref[...]
    v_ref[...] = kv[:, KV_DIM:]


def _kv(x, g, w_kv, k_norm, *, tm):
    rows = x.shape[0]
    out = jax.ShapeDtypeStruct((rows, KV_DIM), F32)
    return pl.pallas_call(
        _kv_body,
        grid=(rows // tm,),
        in_specs=[
            pl.BlockSpec((tm, D_MODEL), lambda i: (i, 0)),
            _resident((1, D_MODEL)),
            _resident((D_MODEL, 2 * KV_DIM)),
            _resident((1, KV_DIM)),
        ],
        out_specs=[pl.BlockSpec((tm, KV_DIM), lambda i: (i, 0))] * 2,
        out_shape=[out, out],
        compiler_params=_params(1),
        name="shared_kv",
    )(x, g.reshape(1, D_MODEL), w_kv, jnp.tile(k_norm, KV_HEADS).reshape(1, KV_DIM))


def _q_body(x_ref, g_ref, w_ref, q_ref, *, tn):
    h = _rms_bf16(x_ref[...], g_ref[...])
    for c0 in range(0, D_MODEL, tn):
        cols = slice(c0, c0 + tn)
        q_ref[:, cols] = jnp.dot(h, w_ref[:, cols], preferred_element_type=F32).astype(BF16)


def _q_proj(x, g, w, *, tm, tn=512):
    rows = x.shape[0]
    return pl.pallas_call(
        functools.partial(_q_body, tn=tn),
        grid=(rows // tm,),
        in_specs=[
            pl.BlockSpec((tm, D_MODEL), lambda i: (i, 0)),
            _resident((1, D_MODEL)),
            _resident((D_MODEL, D_MODEL)),
        ],
        out_specs=pl.BlockSpec((tm, D_MODEL), lambda i: (i, 0)),
        out_shape=jax.ShapeDtypeStruct((rows, D_MODEL), BF16),
        compiler_params=_params(1),
        name="q_proj",
    )(x, g.reshape(1, D_MODEL), w)


def _attn_body(q_ref, kp_ref, kc_ref, km_ref, vp_ref, vc_ref, vm_ref, bias_ref, wq_ref, ones_ref,
               o_ref, *, cb, off):
    c0 = pl.program_id(1) * cb
    pad = jnp.zeros((KEY_PAD, KV_DIM), BF16)
    k_all = jnp.concatenate([kp_ref[0], kc_ref[0]], axis=0).astype(BF16)
    v_all = jnp.concatenate([vp_ref[0], vc_ref[0]], axis=0).astype(BF16)
    k_tail = jnp.concatenate([km_ref[0].astype(BF16), pad], axis=0)
    v_tail = jnp.concatenate([vm_ref[0].astype(BF16), pad], axis=0)
    lane_head = lax.broadcasted_iota(jnp.int32, (1, KV_DIM), 1) // HEAD_DIM
    for i in range(cb):
        rows = slice(i * CHUNK, (i + 1) * CHUNK)
        kc = jnp.concatenate([k_all[i * CHUNK:i * CHUNK + BAND], k_tail], axis=0)
        vc = jnp.concatenate([v_all[i * CHUNK:i * CHUNK + BAND], v_tail], axis=0)
        blocks = []
        for g in range(GROUP):
            qg = q_ref[0, rows, g * KV_DIM:(g + 1) * KV_DIM].astype(F32)
            ss = jnp.dot((qg * qg).astype(BF16), ones_ref[...], preferred_element_type=F32)
            qn = (qg * lax.rsqrt(ss * (1.0 / HEAD_DIM) + EPS) * wq_ref[...]).astype(BF16)
            for j in range(KV_HEADS):
                blocks.append(jnp.where(lane_head == j, qn, jnp.zeros_like(qn)))
        qm = jnp.concatenate(blocks, axis=0)
        logits = lax.dot_general(qm, kc, (((1,), (1,)), ((), ())), preferred_element_type=F32)
        logits = logits + bias_ref[jnp.minimum(c0 + i + off, W_CHUNKS)]
        m = jnp.max(logits, axis=-1, keepdims=True)
        p = jnp.exp(logits - m)
        denom = jnp.sum(p, axis=-1, keepdims=True)
        o = jnp.dot(p.astype(BF16), vc, preferred_element_type=F32) * (1.0 / denom)
        for g in range(GROUP):
            og = jnp.zeros((CHUNK, KV_DIM), F32)
            for j in range(KV_HEADS):
                r0 = (g * KV_HEADS + j) * CHUNK
                og = jnp.where(lane_head == j, o[r0:r0 + CHUNK], og)
            o_ref[0, rows, g * KV_DIM:(g + 1) * KV_DIM] = og.astype(BF16)


def _attention(q, k, v, k_meta, v_meta, bias, wq_row, *, cb, chunk_offset):
    n, t, _ = q.shape
    off = chunk_offset
    assert (cb * CHUNK) % WIN_ROWS == 0 or cb == 1
    prev_spec = pl.BlockSpec(
        (1, WIN_ROWS, KV_DIM),
        lambda i, s: (i, jnp.maximum((s * cb + off) * CHUNK // WIN_ROWS - 1, 0), 0))
    cur_spec = pl.BlockSpec((1, cb * CHUNK, KV_DIM), lambda i, s: (i, s + off // cb, 0))
    meta_spec = pl.BlockSpec((1, N_META, KV_DIM), lambda i, s: (i, 0, 0))
    q_spec = pl.BlockSpec((1, cb * CHUNK, D_MODEL), lambda i, s: (i, s, 0))
    head_ones = (jnp.arange(KV_DIM)[:, None] // HEAD_DIM
                 == jnp.arange(KV_DIM)[None, :] // HEAD_DIM).astype(BF16)
    return pl.pallas_call(
        functools.partial(_attn_body, cb=cb, off=off),
        grid=(n, t // (cb * CHUNK)),
        in_specs=[
            q_spec, prev_spec, cur_spec, meta_spec, prev_spec, cur_spec, meta_spec,
            _resident((W_CHUNKS + 1, N_HEADS * CHUNK, KEY_COLS)),
            _resident((1, KV_DIM)),
            _resident((KV_DIM, KV_DIM)),
        ],
        out_specs=q_spec,
        out_shape=jax.ShapeDtypeStruct((n, t, D_MODEL), BF16),
        compiler_params=_params(2),
        name="window_attention",
    )(q, k, k, k_meta, v, v, v_meta, bias, wq_row, head_ones)


def _t5_bucket(rel):
    nb = N_BUCKETS // 2
    max_exact = nb // 2
    n = jnp.abs(rel)
    large = max_exact + (jnp.log(jnp.maximum(n, 1).astype(jnp.float32) / max_exact)
                         / math.log(MAX_DISTANCE / max_exact) * (nb - max_exact)).astype(jnp.int32)
    large = jnp.minimum(large, nb - 1)
    return jnp.where(rel > 0, nb, 0) + jnp.where(n < max_exact, n, large)


def _stack_heads(b):
    s = b.shape[1]
    b = jnp.moveaxis(b.astype(F32), -1, 0).reshape(KV_HEADS, GROUP, CHUNK, s)
    return jnp.transpose(b, (1, 0, 2, 3)).reshape(N_HEADS * CHUNK, s)


def _bias_tables(table, sink):
    i = jnp.arange(CHUNK)
    s = jnp.arange(BAND)
    m = jnp.arange(N_META)
    band = _stack_heads(table[_t5_bucket(s[None, :] - W_CHUNKS * CHUNK - i[:, None])])
    head_of_block = (jnp.arange(KV_HEADS)[None, :] * GROUP + jnp.arange(GROUP)[:, None]).reshape(-1)
    sink_col = jnp.repeat(sink.astype(F32)[head_of_block], CHUNK)[:, None]
    pad = jnp.full((N_HEADS * CHUNK, KEY_PAD - 1), NEG_INF, F32)
    out = []
    for c in range(W_CHUNKS + 1):
        q_pos = N_META + c * CHUNK + i
        meta = _stack_heads(table[_t5_bucket(m[None, :] - q_pos[:, None])])
        valid = (c - W_CHUNKS + s // CHUNK >= 0)[None, :]
        out.append(jnp.concatenate([jnp.where(valid, band, NEG_INF), meta, sink_col, pad], axis=1))
    return jnp.stack(out)


def _perm_heads_cols(w):
    r = w.shape[0]
    return jnp.transpose(w.reshape(r, KV_HEADS, GROUP, HEAD_DIM), (0, 2, 1, 3)).reshape(r, D_MODEL)


def kernel(x_prompt, x_sample, state_conv, cache_meta_k, cache_meta_v, cache_win_k, cache_win_v,
           meta_tokens, ffn_norm, ffn_w_gate, ffn_w_up, ffn_w_down, conv_norm, conv_w_pw1,
           conv_b_pw1, conv_w_dw, conv_b_dw, conv_ln_g, conv_ln_b, conv_w_pw2, conv_b_pw2,
           kv_norm, w_k, w_v, k_norm, attn_norm, w_q, q_norm, sinks, w_o, rel_bias_table):
    n_p, seq, _ = x_prompt.shape
    n_s, s_new, _ = x_sample.shape
    rows_s = n_s * s_new
    pad_rows = CHUNK - N_META
    tm_ffn = 1024
    tm_big = 512
    tm_small = rows_s + N_META + pad_rows
    n_small = tm_small // s_new

    wg, wu, wd = (w.astype(BF16) for w in (ffn_w_gate, ffn_w_up, ffn_w_down))
    w_pw1 = conv_w_pw1[0].astype(BF16)
    w_pw2 = conv_w_pw2[0].astype(BF16)
    w_kv = jnp.concatenate([w_k, w_v], axis=1).astype(BF16)
    w_qp = _perm_heads_cols(w_q[0]).astype(BF16)
    w_op = _perm_heads_cols(w_o[0].T).T.astype(BF16)

    xb = x_prompt.reshape(n_p * seq, D_MODEL)
    xs = jnp.concatenate([x_sample.reshape(rows_s, D_MODEL), meta_tokens.astype(F32),
                          jnp.zeros((pad_rows, D_MODEL), F32)], axis=0)

    def ffn(x, layer, half, tm):
        return _ffn(x, ffn_norm, wg, wu, wd, layer, half, tm=tm)

    xs = ffn(xs, 0, 0, tm_small)
    xb = ffn(xb, 0, 0, tm_ffn)
    us = _pw1(xs, conv_norm[0], w_pw1, conv_b_pw1[0], tm=tm_small)
    ub = _pw1(xb, conv_norm[0], w_pw1, conv_b_pw1[0], tm=tm_big)

    us3 = us.reshape(n_small, s_new, D_MODEL)
    ub3 = ub.reshape(n_p, seq, D_MODEL)
    zpad = jnp.zeros((n_s, HALO_PAD, D_MODEL), F32)
    hist_s = jnp.concatenate([
        jnp.concatenate([zpad, state_conv[0].astype(F32)], axis=1),
        jnp.zeros((n_small - n_s, HALO, D_MODEL), F32)], axis=0)
    u_meta = us[rows_s:rows_s + N_META]
    hist_b = jnp.concatenate([jnp.zeros((HALO - N_META, D_MODEL), F32), u_meta], axis=0)[None]

    conv_args = (conv_w_dw[0], conv_b_dw[0], conv_ln_g[0], conv_ln_b[0])
    zs = _dwconv(us3, hist_s, *conv_args, tm=s_new).reshape(tm_small, D_MODEL)
    zb = _dwconv(ub3, hist_b, *conv_args, tm=256).reshape(n_p * seq, D_MODEL)
    xs = _mm_res(zs, w_pw2, conv_b_pw2[0], xs, tm=tm_small)
    xb = _mm_res(zb, w_pw2, conv_b_pw2[0], xb, tm=tm_big)
    xs = ffn(xs, 0, 1, tm_small)
    xb = ffn(xb, 0, 1, tm_ffn)

    ks, vs = _kv(xs, kv_norm, w_kv, k_norm, tm=tm_small)
    kb, vb = _kv(xb, kv_norm, w_kv, k_norm, tm=tm_big)
    k_new, v_new = ks[:rows_s], vs[:rows_s]
    k_lead, v_lead = ks[rows_s:rows_s + N_META], vs[rows_s:rows_s + N_META]
    xs = xs[:rows_s]

    xs = ffn(xs, 1, 0, rows_s)
    xb = ffn(xb, 1, 0, tm_ffn)
    qs = _q_proj(xs, attn_norm[0], w_qp, tm=rows_s).reshape(n_s, s_new, D_MODEL)
    qb = _q_proj(xb, attn_norm[0], w_qp, tm=tm_big).reshape(n_p, seq, D_MODEL)

    bias = _bias_tables(rel_bias_table, sinks[0])
    wq_row = (jnp.tile(q_norm[0].astype(F32), KV_HEADS) * SCALE).reshape(1, KV_DIM)

    flat_kv = lambda a: a.reshape(a.shape[0], a.shape[1], KV_DIM).astype(F32)
    k_cat = jnp.concatenate([flat_kv(cache_win_k), k_new.reshape(n_s, s_new, KV_DIM)], axis=1)
    v_cat = jnp.concatenate([flat_kv(cache_win_v), v_new.reshape(n_s, s_new, KV_DIM)], axis=1)
    os_ = _attention(qs, k_cat, v_cat, flat_kv(cache_meta_k), flat_kv(cache_meta_v),
                     bias, wq_row, cb=1, chunk_offset=W_CHUNKS)
    lead = lambda a: jnp.broadcast_to(a[None], (n_p, N_META, KV_DIM))
    ob = _attention(qb, kb.reshape(n_p, seq, KV_DIM), vb.reshape(n_p, seq, KV_DIM),
                    lead(k_lead), lead(v_lead), bias, wq_row, cb=4, chunk_offset=0)

    xs = _mm_res(os_.reshape(rows_s, D_MODEL), w_op, None, xs, tm=rows_s)
    xb = _mm_res(ob.reshape(n_p * seq, D_MODEL), w_op, None, xb, tm=tm_big)
    xs = ffn(xs, 1, 1, rows_s)
    xb = ffn(xb, 1, 1, tm_ffn)

    heads = lambda a, n: a.reshape(n, -1, KV_HEADS, HEAD_DIM)
    keep = CONV_W - 1
    return (
        xb.reshape(n_p, seq, D_MODEL),
        xs.reshape(n_s, s_new, D_MODEL),
        ub3[:, seq - keep:][None],
        heads(lead(k_lead), n_p),
        heads(lead(v_lead), n_p),
        heads(kb.reshape(n_p, seq, KV_DIM)[:, seq - WIN_ROWS:], n_p),
        heads(vb.reshape(n_p, seq, KV_DIM)[:, seq - WIN_ROWS:], n_p),
        us3[:n_s, s_new - keep:][None],
        heads(k_new, n_s),
        heads(v_new, n_s),
    )
```

```python
import functools
import math

import jax
import jax.numpy as jnp
from jax import lax
from jax.experimental import pallas as pl
from jax.experimental.pallas import tpu as pltpu

D_MODEL = 2048
D_FF = 5632
CHUNK = 64
N_META = 16
CONV_W = 31
N_HEADS = 32
HEAD_DIM = 64
KV_HEADS = 4
GROUP = N_HEADS // KV_HEADS
KV_DIM = KV_HEADS * HEAD_DIM
WINDOW = 128
W_CHUNKS = -(-WINDOW // CHUNK)
WIN_ROWS = W_CHUNKS * CHUNK
BAND = (W_CHUNKS + 1) * CHUNK
N_BUCKETS = 32
MAX_DISTANCE = 128
EPS = 1e-6
SCALE = HEAD_DIM ** -0.5
NEG_INF = -1e30
LOG2E = math.log2(math.e)

LANES = 128
SUBLANES = 8
N_SLABS = D_MODEL // LANES
HALO = 32
HALO_PAD = HALO - (CONV_W - 1)
SEG_PAD = 4
KEY_COLS = 256
KEY_PAD = KEY_COLS - BAND - N_META
V7X_VMEM_LIMIT = 56 * 1024 * 1024

F32 = jnp.float32
BF16 = jnp.bfloat16


def _params(n_axes, vmem=V7X_VMEM_LIMIT):
    return pltpu.CompilerParams(
        dimension_semantics=("arbitrary",) * n_axes, vmem_limit_bytes=vmem)


def _resident(shape):
    return pl.BlockSpec(shape, lambda *_: (0,) * len(shape), pipeline_mode=pl.Buffered(1))


def _rms_bf16(x, g):
    ms = jnp.mean(x * x, axis=-1, keepdims=True)
    return (x * lax.rsqrt(ms + EPS) * g).astype(BF16)


def _ffn_body(x_hbm, g_ref, wg_hbm, wu_hbm, wd_hbm, o_ref, xbuf, hbuf, wgb, wub, wdb, xsem, wsem,
              *, layer, half, tm, tf, hc):
    i = pl.program_id(0)
    n_ff = D_FF // tf
    last_tile = i == pl.num_programs(0) - 1
    t0 = i * n_ff
    g = g_ref[...]

    def w_copies(jt, slot):
        cols = pl.ds(jt * tf, tf)
        return (pltpu.make_async_copy(wg_hbm.at[layer, half, :, cols], wgb.at[slot], wsem.at[0, slot]),
                pltpu.make_async_copy(wu_hbm.at[layer, half, :, cols], wub.at[slot], wsem.at[1, slot]),
                pltpu.make_async_copy(wd_hbm.at[layer, half, cols, :], wdb.at[slot], wsem.at[2, slot]))

    def x_copy(tile):
        return pltpu.make_async_copy(x_hbm.at[pl.ds(tile * tm, tm), :], xbuf, xsem.at[0])

    def branch(slot):
        h = hbuf[i % 2]
        gate = jnp.dot(h, wgb[slot], preferred_element_type=F32)
        up = jnp.dot(h, wub[slot], preferred_element_type=F32)
        act = (0.5 * gate * jax.nn.sigmoid(gate) * up).astype(BF16)
        return jnp.dot(act, wdb[slot], preferred_element_type=F32)

    @pl.when(i == 0)
    def _():
        for c in w_copies(0, 0):
            c.start()
        first = x_copy(0)
        first.start()
        first.wait()
        hbuf[0] = _rms_bf16(xbuf[...], g)

    slot0 = t0 % 2
    for c in w_copies(0, slot0):
        c.wait()
    for c in w_copies(1, 1 - slot0):
        c.start()
    o_ref[...] = xbuf[...] + branch(slot0)

    @pl.when(jnp.logical_not(last_tile))
    def _():
        x_copy(i + 1).start()

    def ff_tile(j, carry):
        slot = (t0 + j) % 2
        for c in w_copies(j, slot):
            c.wait()

        @pl.when(jnp.logical_not(jnp.logical_and(last_tile, j == n_ff - 1)))
        def _():
            for c in w_copies((j + 1) % n_ff, 1 - slot):
                c.start()

        @pl.when(jnp.logical_and(j == 1, jnp.logical_not(last_tile)))
        def _():
            x_copy(i + 1).wait()

        o_ref[...] += branch(slot)
        r0 = pl.multiple_of(jnp.minimum((j - 1) * hc, tm - hc), 16)
        hbuf[(i + 1) % 2, pl.ds(r0, hc), :] = _rms_bf16(xbuf[pl.ds(r0, hc), :], g)
        return carry

    lax.fori_loop(1, n_ff, ff_tile, 0)


def _ffn(x, g, wg, wu, wd, layer, half, *, tm, tf=512):
    rows = x.shape[0]
    n_ff = D_FF // tf
    hc = -(-tm // ((n_ff - 1) * 16)) * 16
    assert hc <= tm and (tm - hc) % 16 == 0
    body = functools.partial(_ffn_body, layer=layer, half=half, tm=tm, tf=tf, hc=hc)
    return pl.pallas_call(
        body,
        grid=(rows // tm,),
        in_specs=[
            pl.BlockSpec(memory_space=pl.ANY),
            pl.BlockSpec((None, 1, D_MODEL), lambda i: (layer * 2 + half, 0, 0)),
            pl.BlockSpec(memory_space=pl.ANY),
            pl.BlockSpec(memory_space=pl.ANY),
            pl.BlockSpec(memory_space=pl.ANY),
        ],
        out_specs=pl.BlockSpec((tm, D_MODEL), lambda i: (i, 0)),
        out_shape=jax.ShapeDtypeStruct((rows, D_MODEL), F32),
        scratch_shapes=[
            pltpu.VMEM((tm, D_MODEL), F32),
            pltpu.VMEM((2, tm, D_MODEL), BF16),
            pltpu.VMEM((2, D_MODEL, tf), BF16),
            pltpu.VMEM((2, D_MODEL, tf), BF16),
            pltpu.VMEM((2, tf, D_MODEL), BF16),
            pltpu.SemaphoreType.DMA((1,)),
            pltpu.SemaphoreType.DMA((3, 2)),
        ],
        compiler_params=_params(1),
        name="ffn",
    )(x, g.reshape(-1, 1, D_MODEL), wg, wu, wd)


def _pw1_body(x_ref, g_ref, w_ref, b_ref, u_ref, *, tn):
    h = _rms_bf16(x_ref[...], g_ref[...])
    for c0 in range(0, D_MODEL, tn):
        a = jnp.dot(h, w_ref[:, c0:c0 + tn], preferred_element_type=F32) + b_ref[:, c0:c0 + tn]
        gate = (jnp.dot(h, w_ref[:, D_MODEL + c0:D_MODEL + c0 + tn], preferred_element_type=F32)
                + b_ref[:, D_MODEL + c0:D_MODEL + c0 + tn])
        u_ref[:, c0:c0 + tn] = a * jax.nn.sigmoid(gate)


def _pw1(x, g, w, b, *, tm, tn=512):
    rows = x.shape[0]
    return pl.pallas_call(
        functools.partial(_pw1_body, tn=tn),
        grid=(rows // tm,),
        in_specs=[
            pl.BlockSpec((tm, D_MODEL), lambda i: (i, 0)),
            _resident((1, D_MODEL)),
            _resident((D_MODEL, 2 * D_MODEL)),
            _resident((1, 2 * D_MODEL)),
        ],
        out_specs=pl.BlockSpec((tm, D_MODEL), lambda i: (i, 0)),
        out_shape=jax.ShapeDtypeStruct((rows, D_MODEL), F32),
        compiler_params=_params(1),
        name="pw1_glu",
    )(x, g.reshape(1, D_MODEL), w, b.reshape(1, 2 * D_MODEL))


def _dwconv_body(u_ref, halo_ref, hist_ref, w_ref, b_ref, lg_ref, lb_ref, z_ref,
                 win_ref, y_ref, *, tm):
    seg = tm // SUBLANES
    pitch = seg + SEG_PAD
    n_seg = (HALO + tm) // seg
    q_group = min(seg, SUBLANES)

    halo = jnp.where(pl.program_id(1) == 0, hist_ref[0], halo_ref[0])
    for c in range(N_SLABS):
        lanes = slice(c * LANES, (c + 1) * LANES)
        for sg in range(n_seg):
            r0 = sg * seg
            src = halo[r0:r0 + seg, lanes] if r0 < HALO else u_ref[0, r0 - HALO:r0 - HALO + seg, lanes]
            win_ref[c, sg * pitch:sg * pitch + seg, :] = src

    def slab(c, carry):
        for q0 in range(0, seg, q_group):
            accs = [b_ref[c]] * q_group
            for s in range(HALO_PAD, HALO + 1):
                wv = w_ref[c, s - HALO_PAD]
                for qi in range(q_group):
                    rho = q0 + qi + s
                    start = rho + SEG_PAD * (rho // seg)
                    accs[qi] = accs[qi] + wv * win_ref[c, pl.ds(start, SUBLANES, stride=pitch), :]
            for qi in range(q_group):
                y_ref[c, pl.ds(q0 + qi, SUBLANES, stride=seg), :] = accs[qi]
        return carry

    lax.fori_loop(0, N_SLABS, slab, 0)

    tot = y_ref[0]
    for c in range(1, N_SLABS):
        tot = tot + y_ref[c]
    mu = jnp.sum(tot, axis=-1, keepdims=True) * (1.0 / D_MODEL)
    sq = jnp.zeros((tm, LANES), F32)
    for c in range(N_SLABS):
        d = y_ref[c] - mu
        sq = sq + d * d
    inv = lax.rsqrt(jnp.sum(sq, axis=-1, keepdims=True) * (1.0 / D_MODEL) + EPS)
    for c in range(N_SLABS):
        lanes = slice(c * LANES, (c + 1) * LANES)
        ln = (y_ref[c] - mu) * inv * lg_ref[:, lanes] + lb_ref[:, lanes]
        z_ref[0, :, lanes] = (ln * jax.nn.sigmoid(ln)).astype(BF16)


def _dwconv(u, hist, w_dw, b_dw, ln_g, ln_b, *, tm):
    n, t, _ = u.shape
    per_stream = hist.shape[0] == n
    hb = tm // HALO
    seg = tm // SUBLANES
    win_rows = -(-((HALO + tm) // seg * (seg + SEG_PAD)) // SUBLANES) * SUBLANES
    row = lambda a: a.reshape(1, D_MODEL)
    w_b = jnp.broadcast_to(
        jnp.transpose(w_dw.astype(F32).reshape(CONV_W, N_SLABS, LANES), (1, 0, 2))[:, :, None, :],
        (N_SLABS, CONV_W, SUBLANES, LANES))
    b_b = jnp.broadcast_to(b_dw.astype(F32).reshape(N_SLABS, 1, LANES), (N_SLABS, SUBLANES, LANES))
    return pl.pallas_call(
        functools.partial(_dwconv_body, tm=tm),
        grid=(n, t // tm),
        in_specs=[
            pl.BlockSpec((1, tm, D_MODEL), lambda i, j: (i, j, 0)),
            pl.BlockSpec((1, HALO, D_MODEL), lambda i, j: (i, jnp.maximum(j * hb - 1, 0), 0)),
            pl.BlockSpec((1, HALO, D_MODEL), (lambda i, j: (i, 0, 0)) if per_stream
                         else (lambda i, j: (0, 0, 0))),
            _resident((N_SLABS, CONV_W, SUBLANES, LANES)),
            _resident((N_SLABS, SUBLANES, LANES)),
            _resident((1, D_MODEL)),
            _resident((1, D_MODEL)),
        ],
        out_specs=pl.BlockSpec((1, tm, D_MODEL), lambda i, j: (i, j, 0)),
        out_shape=jax.ShapeDtypeStruct((n, t, D_MODEL), BF16),
        scratch_shapes=[pltpu.VMEM((N_SLABS, win_rows, LANES), F32),
                        pltpu.VMEM((N_SLABS, tm, LANES), F32)],
        compiler_params=_params(2),
        name="dwconv_ln_silu",
    )(u, u, hist, w_b, b_b, row(ln_g), row(ln_b))


def _mm_res_body(z_ref, w_ref, b_ref, x_ref, o_ref, *, tn):
    z = z_ref[...]
    for c0 in range(0, D_MODEL, tn):
        cols = slice(c0, c0 + tn)
        o_ref[:, cols] = (x_ref[:, cols] + jnp.dot(z, w_ref[:, cols], preferred_element_type=F32)
                          + b_ref[:, cols])


def _mm_res_nobias_body(z_ref, w_ref, x_ref, o_ref, *, tn):
    z = z_ref[...]
    for c0 in range(0, D_MODEL, tn):
        cols = slice(c0, c0 + tn)
        o_ref[:, cols] = x_ref[:, cols] + jnp.dot(z, w_ref[:, cols], preferred_element_type=F32)


def _mm_res(z, w, b, x, *, tm, tn=512):
    rows = x.shape[0]
    row_spec = pl.BlockSpec((tm, D_MODEL), lambda i: (i, 0))
    w_spec = _resident((D_MODEL, D_MODEL))
    if b is None:
        body, in_specs, args = _mm_res_nobias_body, [row_spec, w_spec, row_spec], (z, w, x)
    else:
        body = _mm_res_body
        in_specs = [row_spec, w_spec, _resident((1, D_MODEL)), row_spec]
        args = (z, w, b.reshape(1, D_MODEL), x)
    return pl.pallas_call(
        functools.partial(body, tn=tn),
        grid=(rows // tm,),
        in_specs=in_specs,
        out_specs=row_spec,
        out_shape=jax.ShapeDtypeStruct((rows, D_MODEL), F32),
        compiler_params=_params(1),
        name="matmul_residual",
    )(*args)


def _kv_body(x_ref, g_ref, w_ref, kn_ref, k_ref, v_ref):
    h = _rms_bf16(x_ref[...], g_ref[...])
    kv = jnp.dot(h, w_ref[...], preferred_element_type=F32)
    k = kv[:, :KV_DIM]
    lane_head = lax.broadcasted_iota(jnp.int32, (1, KV_DIM), 1) // HEAD_DIM
    inv = jnp.zeros_like(k)
    for j in range(KV_HEADS):
        sel = lane_head == j
        ss = jnp.sum(jnp.where(sel, k * k, 0.0), axis=-1, keepdims=True)
        inv = jnp.where(sel, lax.rsqrt(ss * (1.0 / HEAD_DIM) + EPS), inv)
    k_ref[...] = k * inv * kn_ref[...]
    v_ref[...] = kv[:, KV_DIM:]


def _kv(x, g, w_kv, k_norm, *, tm):
    rows = x.shape[0]
    out = jax.ShapeDtypeStruct((rows, KV_DIM), F32)
    return pl.pallas_call(
        _kv_body,
        grid=(rows // tm,),
        in_specs=[
            pl.BlockSpec((tm, D_MODEL), lambda i: (i, 0)),
            _resident((1, D_MODEL)),
            _resident((D_MODEL, 2 * KV_DIM)),
            _resident((1, KV_DIM)),
        ],
        out_specs=[pl.BlockSpec((tm, KV_DIM), lambda i: (i, 0))] * 2,
        out_shape=[out, out],
        compiler_params=_params(1),
        name="shared_kv",
    )(x, g.reshape(1, D_MODEL), w_kv, jnp.tile(k_norm, KV_HEADS).reshape(1, KV_DIM))


def _q_body(x_ref, g_ref, w_ref, q_ref, *, tn):
    h = _rms_bf16(x_ref[...], g_ref[...])
    for c0 in range(0, D_MODEL, tn):
        cols = slice(c0, c0 + tn)
        q_ref[:, cols] = jnp.dot(h, w_ref[:, cols], preferred_element_type=F32).astype(BF16)


def _q_proj(x, g, w, *, tm, tn=512):
    rows = x.shape[0]
    return pl.pallas_call(
        functools.partial(_q_body, tn=tn),
        grid=(rows // tm,),
        in_specs=[
            pl.BlockSpec((tm, D_MODEL), lambda i: (i, 0)),
            _resident((1, D_MODEL)),
            _resident((D_MODEL, D_MODEL)),
        ],
        out_specs=pl.BlockSpec((tm, D_MODEL), lambda i: (i, 0)),
        out_shape=jax.ShapeDtypeStruct((rows, D_MODEL), BF16),
        compiler_params=_params(1),
        name="q_proj",
    )(x, g.reshape(1, D_MODEL), w)


def _attn_body(q_ref, kp_ref, kc_ref, km_ref, vp_ref, vc_ref, vm_ref, bias_ref, wq_ref, ones_ref,
               o_ref, *, cb, off):
    c0 = pl.program_id(1) * cb
    pad = jnp.zeros((KEY_PAD, KV_DIM), BF16)
    k_all = jnp.concatenate([kp_ref[0], kc_ref[0]], axis=0).astype(BF16)
    v_all = jnp.concatenate([vp_ref[0], vc_ref[0]], axis=0).astype(BF16)
    k_tail = jnp.concatenate([km_ref[0].astype(BF16), pad], axis=0)
    v_tail = jnp.concatenate([vm_ref[0].astype(BF16), pad], axis=0)
    lane_head = lax.broadcasted_iota(jnp.int32, (1, KV_DIM), 1) // HEAD_DIM
    zero = jnp.zeros((CHUNK, KV_DIM), BF16)
    for i in range(cb):
        rows = slice(i * CHUNK, (i + 1) * CHUNK)
        kc = jnp.concatenate([k_all[i * CHUNK:i * CHUNK + BAND], k_tail], axis=0)
        vc = jnp.concatenate([v_all[i * CHUNK:i * CHUNK + BAND], v_tail], axis=0)
        qn = []
        for g in range(GROUP):
            qg = q_ref[0, rows, g * KV_DIM:(g + 1) * KV_DIM].astype(F32)
            ss = jnp.dot((qg * qg).astype(BF16), ones_ref[...], preferred_element_type=F32)
            qn.append((qg * lax.rsqrt(ss * (1.0 / HEAD_DIM) + EPS) * wq_ref[...]).astype(BF16))
        qm = jnp.concatenate([jnp.where(lane_head == j, qn[g], zero)
                              for j in range(KV_HEADS) for g in range(GROUP)], axis=0)
        logits = lax.dot_general(qm, kc, (((1,), (1,)), ((), ())), preferred_element_type=F32)
        logits = logits + bias_ref[jnp.minimum(c0 + i + off, W_CHUNKS)]
        p = jnp.exp2(logits - jnp.max(logits, axis=-1, keepdims=True))
        denom = jnp.sum(p, axis=-1, keepdims=True)
        o = jnp.dot(p.astype(BF16), vc, preferred_element_type=F32) * (1.0 / denom)
        for g in range(GROUP):
            og = jnp.zeros((CHUNK, KV_DIM), F32)
            for j in range(KV_HEADS):
                r0 = (j * GROUP + g) * CHUNK
                og = jnp.where(lane_head == j, o[r0:r0 + CHUNK], og)
            o_ref[0, rows, g * KV_DIM:(g + 1) * KV_DIM] = og.astype(BF16)


def _attention(q, k, v, k_meta, v_meta, bias, wq_row, *, cb, chunk_offset):
    n, t, _ = q.shape
    off = chunk_offset
    assert (cb * CHUNK) % WIN_ROWS == 0 or cb == 1
    prev_spec = pl.BlockSpec(
        (1, WIN_ROWS, KV_DIM),
        lambda i, s: (i, jnp.maximum((s * cb + off) * CHUNK // WIN_ROWS - 1, 0), 0))
    cur_spec = pl.BlockSpec((1, cb * CHUNK, KV_DIM), lambda i, s: (i, s + off // cb, 0))
    meta_spec = pl.BlockSpec((1, N_META, KV_DIM), lambda i, s: (i, 0, 0))
    q_spec = pl.BlockSpec((1, cb * CHUNK, D_MODEL), lambda i, s: (i, s, 0))
    head_ones = (jnp.arange(KV_DIM)[:, None] // HEAD_DIM
                 == jnp.arange(KV_DIM)[None, :] // HEAD_DIM).astype(BF16)
    return pl.pallas_call(
        functools.partial(_attn_body, cb=cb, off=off),
        grid=(n, t // (cb * CHUNK)),
        in_specs=[
            q_spec, prev_spec, cur_spec, meta_spec, prev_spec, cur_spec, meta_spec,
            _resident((W_CHUNKS + 1, N_HEADS * CHUNK, KEY_COLS)),
            _resident((1, KV_DIM)),
            _resident((KV_DIM, KV_DIM)),
        ],
        out_specs=q_spec,
        out_shape=jax.ShapeDtypeStruct((n, t, D_MODEL), BF16),
        compiler_params=_params(2),
        name="window_attention",
    )(q, k, k, k_meta, v, v, v_meta, bias, wq_row, head_ones)


def _t5_bucket(rel):
    nb = N_BUCKETS // 2
    max_exact = nb // 2
    n = jnp.abs(rel)
    large = max_exact + (jnp.log(jnp.maximum(n, 1).astype(jnp.float32) / max_exact)
                         / math.log(MAX_DISTANCE / max_exact) * (nb - max_exact)).astype(jnp.int32)
    large = jnp.minimum(large, nb - 1)
    return jnp.where(rel > 0, nb, 0) + jnp.where(n < max_exact, n, large)


def _stack_heads(b):
    return jnp.moveaxis(b.astype(F32), -1, 0).reshape(N_HEADS * CHUNK, b.shape[1])


def _bias_tables(table, sink):
    i = jnp.arange(CHUNK)
    s = jnp.arange(BAND)
    m = jnp.arange(N_META)
    band = _stack_heads(table[_t5_bucket(s[None, :] - W_CHUNKS * CHUNK - i[:, None])])
    sink_col = jnp.repeat(sink.astype(F32), CHUNK)[:, None]
    pad = jnp.full((N_HEADS * CHUNK, KEY_PAD - 1), NEG_INF, F32)
    out = []
    for c in range(W_CHUNKS + 1):
        q_pos = N_META + c * CHUNK + i
        meta = _stack_heads(table[_t5_bucket(m[None, :] - q_pos[:, None])])
        valid = (c - W_CHUNKS + s // CHUNK >= 0)[None, :]
        out.append(jnp.concatenate([jnp.where(valid, band, NEG_INF), meta, sink_col, pad], axis=1))
    return jnp.stack(out) * LOG2E


def _perm_heads_cols(w):
    r = w.shape[0]
    return jnp.transpose(w.reshape(r, KV_HEADS, GROUP, HEAD_DIM), (0, 2, 1, 3)).reshape(r, D_MODEL)


def kernel(x_prompt, x_sample, state_conv, cache_meta_k, cache_meta_v, cache_win_k, cache_win_v,
           meta_tokens, ffn_norm, ffn_w_gate, ffn_w_up, ffn_w_down, conv_norm, conv_w_pw1,
           conv_b_pw1, conv_w_dw, conv_b_dw, conv_ln_g, conv_ln_b, conv_w_pw2, conv_b_pw2,
           kv_norm, w_k, w_v, k_norm, attn_norm, w_q, q_norm, sinks, w_o, rel_bias_table):
    n_p, seq, _ = x_prompt.shape
    n_s, s_new, _ = x_sample.shape
    rows_s = n_s * s_new
    pad_rows = CHUNK - N_META
    tm_ffn = 1024
    tm_big = 512
    tm_small = rows_s + N_META + pad_rows
    n_small = tm_small // s_new

    wg, wu, wd = (w.astype(BF16) for w in (ffn_w_gate, ffn_w_up, ffn_w_down))
    w_pw1 = conv_w_pw1[0].astype(BF16)
    w_pw2 = conv_w_pw2[0].astype(BF16)
    w_kv = jnp.concatenate([w_k, w_v], axis=1).astype(BF16)
    w_qp = _perm_heads_cols(w_q[0]).astype(BF16)
    w_op = _perm_heads_cols(w_o[0].T).T.astype(BF16)

    xb = x_prompt.reshape(n_p * seq, D_MODEL)
    xs = jnp.concatenate([x_sample.reshape(rows_s, D_MODEL), meta_tokens.astype(F32),
                          jnp.zeros((pad_rows, D_MODEL), F32)], axis=0)

    def ffn(x, layer, half, tm):
        return _ffn(x, ffn_norm, wg, wu, wd, layer, half, tm=tm)

    xs = ffn(xs, 0, 0, tm_small)
    xb = ffn(xb, 0, 0, tm_ffn)
    us = _pw1(xs, conv_norm[0], w_pw1, conv_b_pw1[0], tm=tm_small)
    ub = _pw1(xb, conv_norm[0], w_pw1, conv_b_pw1[0], tm=tm_big)

    us3 = us.reshape(n_small, s_new, D_MODEL)
    ub3 = ub.reshape(n_p, seq, D_MODEL)
    zpad = jnp.zeros((n_s, HALO_PAD, D_MODEL), F32)
    hist_s = jnp.concatenate([
        jnp.concatenate([zpad, state_conv[0].astype(F32)], axis=1),
        jnp.zeros((n_small - n_s, HALO, D_MODEL), F32)], axis=0)
    u_meta = us[rows_s:rows_s + N_META]
    hist_b = jnp.concatenate([jnp.zeros((HALO - N_META, D_MODEL), F32), u_meta], axis=0)[None]

    conv_args = (conv_w_dw[0], conv_b_dw[0], conv_ln_g[0], conv_ln_b[0])
    zs = _dwconv(us3, hist_s, *conv_args, tm=s_new).reshape(tm_small, D_MODEL)
    zb = _dwconv(ub3, hist_b, *conv_args, tm=256).reshape(n_p * seq, D_MODEL)
    xs = _mm_res(zs, w_pw2, conv_b_pw2[0], xs, tm=tm_small)
    xb = _mm_res(zb, w_pw2, conv_b_pw2[0], xb, tm=tm_big)
    xs = ffn(xs, 0, 1, tm_small)
    xb = ffn(xb, 0, 1, tm_ffn)

    ks, vs = _kv(xs, kv_norm, w_kv, k_norm, tm=tm_small)
    kb, vb = _kv(xb, kv_norm, w_kv, k_norm, tm=tm_big)
    k_new, v_new = ks[:rows_s], vs[:rows_s]
    k_lead, v_lead = ks[rows_s:rows_s + N_META], vs[rows_s:rows_s + N_META]
    xs = xs[:rows_s]

    xs = ffn(xs, 1, 0, rows_s)
    xb = ffn(xb, 1, 0, tm_ffn)
    qs = _q_proj(xs, attn_norm[0], w_qp, tm=rows_s).reshape(n_s, s_new, D_MODEL)
    qb = _q_proj(xb, attn_norm[0], w_qp, tm=tm_big).reshape(n_p, seq, D_MODEL)

    bias = _bias_tables(rel_bias_table, sinks[0])
    wq_row = (jnp.tile(q_norm[0].astype(F32), KV_HEADS) * (SCALE * LOG2E)).reshape(1, KV_DIM)

    flat_kv = lambda a: a.reshape(a.shape[0], a.shape[1], KV_DIM).astype(F32)
    k_cat = jnp.concatenate([flat_kv(cache_win_k), k_new.reshape(n_s, s_new, KV_DIM)], axis=1)
    v_cat = jnp.concatenate([flat_kv(cache_win_v), v_new.reshape(n_s, s_new, KV_DIM)], axis=1)
    os_ = _attention(qs, k_cat, v_cat, flat_kv(cache_meta_k), flat_kv(cache_meta_v),
                     bias, wq_row, cb=1, chunk_offset=W_CHUNKS)
    lead = lambda a: jnp.broadcast_to(a[None], (n_p, N_META, KV_DIM))
    ob = _attention(qb, kb.reshape(n_p, seq, KV_DIM), vb.reshape(n_p, seq, KV_DIM),
                    lead(k_lead), lead(v_lead), bias, wq_row, cb=8, chunk_offset=0)

    xs = _mm_res(os_.reshape(rows_s, D_MODEL), w_op, None, xs, tm=rows_s)
    xb = _mm_res(ob.reshape(n_p * seq, D_MODEL), w_op, None, xb, tm=tm_big)
    xs = ffn(xs, 1, 1, rows_s)
    xb = ffn(xb, 1, 1, tm_ffn)

    heads = lambda a, n: a.reshape(n, -1, KV_HEADS, HEAD_DIM)
    keep = CONV_W - 1
    return (
        xb.reshape(n_p, seq, D_MODEL),
        xs.reshape(n_s, s_new, D_MODEL),
        ub3[:, seq - keep:][None],
        heads(lead(k_lead), n_p),
        heads(lead(v_lead), n_p),
        heads(kb.reshape(n_p, seq, KV_DIM)[:, seq - WIN_ROWS:], n_p),
        heads(vb.reshape(n_p, seq, KV_DIM)[:, seq - WIN_ROWS:], n_p),
        us3[:n_s, s_new - keep:][None],
        heads(k_new, n_s),
        heads(v_new, n_s),
    )
```

```python
import functools
import math

import jax
import jax.numpy as jnp
from jax import lax
from jax.experimental import pallas as pl
from jax.experimental.pallas import tpu as pltpu

D_MODEL = 2048
D_FF = 5632
CHUNK = 64
N_META = 16
CONV_W = 31
N_HEADS = 32
HEAD_DIM = 64
KV_HEADS = 4
GROUP = N_HEADS // KV_HEADS
KV_DIM = KV_HEADS * HEAD_DIM
WINDOW = 128
W_CHUNKS = -(-WINDOW // CHUNK)
WIN_ROWS = W_CHUNKS * CHUNK
BAND = (W_CHUNKS + 1) * CHUNK
N_BUCKETS = 32
MAX_DISTANCE = 128
EPS = 1e-6
SCALE = HEAD_DIM ** -0.5
NEG_INF = -1e30
LOG2E = math.log2(math.e)

LANES = 128
SUBLANES = 8
N_SLABS = D_MODEL // LANES
HALO = 32
HALO_PAD = HALO - (CONV_W - 1)
SEG_PAD = 4
KEY_COLS = 256
KEY_PAD = KEY_COLS - BAND - N_META
V7X_VMEM_LIMIT = 56 * 1024 * 1024

F32 = jnp.float32
BF16 = jnp.bfloat16


def _params(n_axes, vmem=V7X_VMEM_LIMIT):
    return pltpu.CompilerParams(
        dimension_semantics=("arbitrary",) * n_axes, vmem_limit_bytes=vmem)


def _resident(shape):
    return pl.BlockSpec(shape, lambda *_: (0,) * len(shape), pipeline_mode=pl.Buffered(1))


def _rms_bf16(x, g):
    ms = jnp.mean(x * x, axis=-1, keepdims=True)
    return (x * lax.rsqrt(ms + EPS) * g).astype(BF16)


FF_TILE = 512


def _ffn_body(x_ref, g_ref, wgu_ref, wd_ref, o_ref, h_ref):
    @pl.when(pl.program_id(1) == 0)
    def _():
        x = x_ref[...]
        h_ref[...] = _rms_bf16(x, g_ref[...])
        o_ref[...] = x

    gu = jnp.dot(h_ref[...], wgu_ref[...], preferred_element_type=F32)
    gate, up = gu[:, :FF_TILE], gu[:, FF_TILE:]
    act = (0.5 * gate * jax.nn.sigmoid(gate) * up).astype(BF16)
    o_ref[...] += jnp.dot(act, wd_ref[...], preferred_element_type=F32)


def _ffn_weights(w_gate, w_up, w_down):
    lead = w_gate.shape[:2]
    n_ff = D_FF // FF_TILE
    tiles = lambda w: w.astype(BF16).reshape(*lead, D_MODEL, n_ff, FF_TILE)
    wgu = jnp.concatenate([tiles(w_gate), tiles(w_up)], axis=-1)
    return jnp.transpose(wgu, (0, 1, 3, 2, 4)), w_down.astype(BF16)


def _ffn(x, g, wgu, wd, layer, half, *, tm):
    rows = x.shape[0]
    return pl.pallas_call(
        _ffn_body,
        grid=(rows // tm, D_FF // FF_TILE),
        in_specs=[
            pl.BlockSpec((tm, D_MODEL), lambda i, j: (i, 0)),
            pl.BlockSpec((None, 1, D_MODEL), lambda i, j: (layer * 2 + half, 0, 0)),
            pl.BlockSpec((None, None, None, D_MODEL, 2 * FF_TILE),
                         lambda i, j: (layer, half, j, 0, 0)),
            pl.BlockSpec((None, None, FF_TILE, D_MODEL), lambda i, j: (layer, half, j, 0)),
        ],
        out_specs=pl.BlockSpec((tm, D_MODEL), lambda i, j: (i, 0)),
        out_shape=jax.ShapeDtypeStruct((rows, D_MODEL), F32),
        scratch_shapes=[pltpu.VMEM((tm, D_MODEL), BF16)],
        compiler_params=_params(2),
        name="ffn",
    )(x, g.reshape(-1, 1, D_MODEL), wgu, wd)


def _pw1_body(x_ref, g_ref, w_ref, b_ref, u_ref, *, tn):
    h = _rms_bf16(x_ref[...], g_ref[...])
    for c0 in range(0, D_MODEL, tn):
        a = jnp.dot(h, w_ref[:, c0:c0 + tn], preferred_element_type=F32) + b_ref[:, c0:c0 + tn]
        gate = (jnp.dot(h, w_ref[:, D_MODEL + c0:D_MODEL + c0 + tn], preferred_element_type=F32)
                + b_ref[:, D_MODEL + c0:D_MODEL + c0 + tn])
        u_ref[:, c0:c0 + tn] = a * jax.nn.sigmoid(gate)


def _pw1(x, g, w, b, *, tm, tn=512):
    rows = x.shape[0]
    return pl.pallas_call(
        functools.partial(_pw1_body, tn=tn),
        grid=(rows // tm,),
        in_specs=[
            pl.BlockSpec((tm, D_MODEL), lambda i: (i, 0)),
            _resident((1, D_MODEL)),
            _resident((D_MODEL, 2 * D_MODEL)),
            _resident((1, 2 * D_MODEL)),
        ],
        out_specs=pl.BlockSpec((tm, D_MODEL), lambda i: (i, 0)),
        out_shape=jax.ShapeDtypeStruct((rows, D_MODEL), F32),
        compiler_params=_params(1),
        name="pw1_glu",
    )(x, g.reshape(1, D_MODEL), w, b.reshape(1, 2 * D_MODEL))


def _conv_taps(win_ref, w_ref, b_ref, y_ref, c, *, seg, hist_rows, q_group=SUBLANES):
    pitch = seg + SEG_PAD
    q_group = min(seg, q_group)
    for q0 in range(0, seg, q_group):
        accs = [b_ref[c]] * q_group
        for k in range(CONV_W):
            wv = w_ref[c, k]
            for qi in range(q_group):
                rho = q0 + qi + hist_rows - (CONV_W - 1) + k
                start = rho + SEG_PAD * (rho // seg)
                accs[qi] = accs[qi] + wv * win_ref[c, pl.ds(start, SUBLANES, stride=pitch), :]
        for qi in range(q_group):
            y_ref[c, pl.ds(q0 + qi, SUBLANES, stride=seg), :] = accs[qi]


def _ln_silu_store(y_ref, lg_ref, lb_ref, z_ref, *, tm):
    tot = y_ref[0]
    for c in range(1, N_SLABS):
        tot = tot + y_ref[c]
    mu = jnp.sum(tot, axis=-1, keepdims=True) * (1.0 / D_MODEL)
    sq = jnp.zeros((tm, LANES), F32)
    for c in range(N_SLABS):
        d = y_ref[c] - mu
        sq = sq + d * d
    inv = lax.rsqrt(jnp.sum(sq, axis=-1, keepdims=True) * (1.0 / D_MODEL) + EPS)
    for c in range(N_SLABS):
        lanes = slice(c * LANES, (c + 1) * LANES)
        ln = (y_ref[c] - mu) * inv * lg_ref[:, lanes] + lb_ref[:, lanes]
        z_ref[0, :, lanes] = (ln * jax.nn.sigmoid(ln)).astype(BF16)


def _dwconv_body(u_ref, halo_ref, hist_ref, w_ref, b_ref, lg_ref, lb_ref, z_ref,
                 win_ref, y_ref, *, tm):
    seg = tm // SUBLANES
    pitch = seg + SEG_PAD
    halo = jnp.where(pl.program_id(1) == 0, hist_ref[0], halo_ref[0])
    for c in range(N_SLABS):
        lanes = slice(c * LANES, (c + 1) * LANES)
        for sg in range((HALO + tm) // seg):
            r0 = sg * seg
            src = halo[r0:r0 + seg, lanes] if r0 < HALO else u_ref[0, r0 - HALO:r0 - HALO + seg, lanes]
            win_ref[c, sg * pitch:sg * pitch + seg, :] = src

    def slab(c, carry):
        _conv_taps(win_ref, w_ref, b_ref, y_ref, c, seg=seg, hist_rows=HALO)
        return carry

    lax.fori_loop(0, N_SLABS, slab, 0)
    _ln_silu_store(y_ref, lg_ref, lb_ref, z_ref, tm=tm)


def _conv_tables(w_dw, b_dw):
    w_b = jnp.broadcast_to(
        jnp.transpose(w_dw.astype(F32).reshape(CONV_W, N_SLABS, LANES), (1, 0, 2))[:, :, None, :],
        (N_SLABS, CONV_W, SUBLANES, LANES))
    b_b = jnp.broadcast_to(b_dw.astype(F32).reshape(N_SLABS, 1, LANES), (N_SLABS, SUBLANES, LANES))
    return w_b, b_b


def _win_rows(hist_rows, tm):
    seg = tm // SUBLANES
    return -(-((hist_rows + tm) // seg * (seg + SEG_PAD)) // SUBLANES) * SUBLANES


def _dwconv(u, hist, w_dw, b_dw, ln_g, ln_b, *, tm):
    n, t, _ = u.shape
    per_stream = hist.shape[0] == n
    hb = tm // HALO
    win_rows = _win_rows(HALO, tm)
    row = lambda a: a.reshape(1, D_MODEL)
    w_b, b_b = _conv_tables(w_dw, b_dw)
    return pl.pallas_call(
        functools.partial(_dwconv_body, tm=tm),
        grid=(n, t // tm),
        in_specs=[
            pl.BlockSpec((1, tm, D_MODEL), lambda i, j: (i, j, 0)),
            pl.BlockSpec((1, HALO, D_MODEL), lambda i, j: (i, jnp.maximum(j * hb - 1, 0), 0)),
            pl.BlockSpec((1, HALO, D_MODEL), (lambda i, j: (i, 0, 0)) if per_stream
                         else (lambda i, j: (0, 0, 0))),
            _resident((N_SLABS, CONV_W, SUBLANES, LANES)),
            _resident((N_SLABS, SUBLANES, LANES)),
            _resident((1, D_MODEL)),
            _resident((1, D_MODEL)),
        ],
        out_specs=pl.BlockSpec((1, tm, D_MODEL), lambda i, j: (i, j, 0)),
        out_shape=jax.ShapeDtypeStruct((n, t, D_MODEL), BF16),
        scratch_shapes=[pltpu.VMEM((N_SLABS, win_rows, LANES), F32),
                        pltpu.VMEM((N_SLABS, tm, LANES), F32)],
        compiler_params=_params(2),
        name="dwconv_ln_silu",
    )(u, u, hist, w_b, b_b, row(ln_g), row(ln_b))


GLU_COLS = 2 * LANES


def _conv_front_body(x_ref, hist_ref, g_ref, w_ref, b_ref, wdw_ref, bdw_ref, lg_ref, lb_ref,
                     z_ref, tail_ref, win_ref, y_ref, carry_ref, h_ref, *, tm):
    seg = tm // SUBLANES
    pitch = seg + SEG_PAD
    @pl.when(pl.program_id(1) == 0)
    def _():
        for c in range(N_SLABS):
            carry_ref[c] = hist_ref[0, :, c * LANES:(c + 1) * LANES]

    h_ref[...] = _rms_bf16(x_ref[0], g_ref[...])
    for c in range(N_SLABS):
        win_ref[c, 0:seg, :] = carry_ref[c]

    def glu_block(k):
        ag = jnp.dot(h_ref[...], w_ref[k], preferred_element_type=F32) + b_ref[k]
        u = ag[:, :GLU_COLS] * jax.nn.sigmoid(ag[:, GLU_COLS:])
        tail_ref[0, :, k * GLU_COLS:(k + 1) * GLU_COLS] = u[tm - HALO:]
        for e in range(GLU_COLS // LANES):
            c = k * (GLU_COLS // LANES) + e
            ue = u[:, e * LANES:(e + 1) * LANES]
            for sg in range(SUBLANES):
                win_ref[c, (sg + 1) * pitch:(sg + 1) * pitch + seg, :] = ue[sg * seg:(sg + 1) * seg]
            carry_ref[c] = ue[tm - seg:]

    def taps_block(k):
        for e in range(GLU_COLS // LANES):
            _conv_taps(win_ref, wdw_ref, bdw_ref, y_ref, k * (GLU_COLS // LANES) + e,
                       seg=seg, hist_rows=seg, q_group=4)

    always = pl.program_id(1) >= 0
    n_blocks = D_MODEL // GLU_COLS
    glu_block(0)
    for k in range(1, n_blocks):
        @pl.when(always)
        def _(k=k):
            taps_block(k - 1)
            glu_block(k)
    taps_block(n_blocks - 1)
    _ln_silu_store(y_ref, lg_ref, lb_ref, z_ref, tm=tm)


def _conv_front(x, hist, g, w_pw1, b_pw1, w_dw, b_dw, ln_g, ln_b, *, tm):
    n, t, _ = x.shape
    seg = tm // SUBLANES
    n_blocks = D_MODEL // GLU_COLS
    row = lambda a: a.reshape(1, D_MODEL)
    w_b, b_b = _conv_tables(w_dw, b_dw)
    wk = jnp.concatenate([w_pw1[:, :D_MODEL].reshape(D_MODEL, n_blocks, GLU_COLS),
                          w_pw1[:, D_MODEL:].reshape(D_MODEL, n_blocks, GLU_COLS)], axis=2)
    wk = jnp.transpose(wk, (1, 0, 2))
    bk = jnp.concatenate([b_pw1[:D_MODEL].reshape(n_blocks, 1, GLU_COLS),
                          b_pw1[D_MODEL:].reshape(n_blocks, 1, GLU_COLS)], axis=2).astype(F32)
    return pl.pallas_call(
        functools.partial(_conv_front_body, tm=tm),
        grid=(n, t // tm),
        in_specs=[
            pl.BlockSpec((1, tm, D_MODEL), lambda i, j: (i, j, 0)),
            _resident((1, seg, D_MODEL)),
            _resident((1, D_MODEL)),
            _resident((n_blocks, D_MODEL, 2 * GLU_COLS)),
            _resident((n_blocks, 1, 2 * GLU_COLS)),
            _resident((N_SLABS, CONV_W, SUBLANES, LANES)),
            _resident((N_SLABS, SUBLANES, LANES)),
            _resident((1, D_MODEL)),
            _resident((1, D_MODEL)),
        ],
        out_specs=[pl.BlockSpec((1, tm, D_MODEL), lambda i, j: (i, j, 0)),
                   pl.BlockSpec((1, HALO, D_MODEL), lambda i, j: (i, 0, 0))],
        out_shape=[jax.ShapeDtypeStruct((n, t, D_MODEL), BF16),
                   jax.ShapeDtypeStruct((n, HALO, D_MODEL), F32)],
        scratch_shapes=[pltpu.VMEM((N_SLABS, _win_rows(seg, tm), LANES), F32),
                        pltpu.VMEM((N_SLABS, tm, LANES), F32),
                        pltpu.VMEM((N_SLABS, seg, LANES), F32),
                        pltpu.VMEM((tm, D_MODEL), BF16)],
        compiler_params=_params(2),
        name="conv_front",
    )(x, hist, row(g), wk, bk, w_b, b_b, row(ln_g), row(ln_b))


def _mm_res_body(z_ref, w_ref, b_ref, x_ref, o_ref, *, tn):
    z = z_ref[...]
    for c0 in range(0, D_MODEL, tn):
        cols = slice(c0, c0 + tn)
        o_ref[:, cols] = (x_ref[:, cols] + jnp.dot(z, w_ref[:, cols], preferred_element_type=F32)
                          + b_ref[:, cols])


def _mm_res_nobias_body(z_ref, w_ref, x_ref, o_ref, *, tn):
    z = z_ref[...]
    for c0 in range(0, D_MODEL, tn):
        cols = slice(c0, c0 + tn)
        o_ref[:, cols] = x_ref[:, cols] + jnp.dot(z, w_ref[:, cols], preferred_element_type=F32)


def _mm_res(z, w, b, x, *, tm, tn=512):
    rows = x.shape[0]
    row_spec = pl.BlockSpec((tm, D_MODEL), lambda i: (i, 0))
    w_spec = _resident((D_MODEL, D_MODEL))
    if b is None:
        body, in_specs, args = _mm_res_nobias_body, [row_spec, w_spec, row_spec], (z, w, x)
    else:
        body = _mm_res_body
        in_specs = [row_spec, w_spec, _resident((1, D_MODEL)), row_spec]
        args = (z, w, b.reshape(1, D_MODEL), x)
    return pl.pallas_call(
        functools.partial(body, tn=tn),
        grid=(rows // tm,),
        in_specs=in_specs,
        out_specs=row_spec,
        out_shape=jax.ShapeDtypeStruct((rows, D_MODEL), F32),
        compiler_params=_params(1),
        name="matmul_residual",
    )(*args)


def _kv_body(x_ref, g_ref, w_ref, kn_ref, k_ref, v_ref):
    h = _rms_bf16(x_ref[...], g_ref[...])
    kv = jnp.dot(h, w_ref[...], preferred_element_type=F32)
    k = kv[:, :KV_DIM]
    lane_head = lax.broadcasted_iota(jnp.int32, (1, KV_DIM), 1) // HEAD_DIM
    inv = jnp.zeros_like(k)
    for j in range(KV_HEADS):
        sel = lane_head == j
        ss = jnp.sum(jnp.where(sel, k * k, 0.0), axis=-1, keepdims=True)
        inv = jnp.where(sel, lax.rsqrt(ss * (1.0 / HEAD_DIM) + EPS), inv)
    k_ref[...] = k * inv * kn_ref[...]
    v_ref[...] = kv[:, KV_DIM:]


def _kv(x, g, w_kv, k_norm, *, tm):
    rows = x.shape[0]
    out = jax.ShapeDtypeStruct((rows, KV_DIM), F32)
    return pl.pallas_call(
        _kv_body,
        grid=(rows // tm,),
        in_specs=[
            pl.BlockSpec((tm, D_MODEL), lambda i: (i, 0)),
            _resident((1, D_MODEL)),
            _resident((D_MODEL, 2 * KV_DIM)),
            _resident((1, KV_DIM)),
        ],
        out_specs=[pl.BlockSpec((tm, KV_DIM), lambda i: (i, 0))] * 2,
        out_shape=[out, out],
        compiler_params=_params(1),
        name="shared_kv",
    )(x, g.reshape(1, D_MODEL), w_kv, jnp.tile(k_norm, KV_HEADS).reshape(1, KV_DIM))


def _q_body(x_ref, g_ref, w_ref, q_ref, *, tn):
    h = _rms_bf16(x_ref[...], g_ref[...])
    for c0 in range(0, D_MODEL, tn):
        cols = slice(c0, c0 + tn)
        q_ref[:, cols] = jnp.dot(h, w_ref[:, cols], preferred_element_type=F32).astype(BF16)


def _q_proj(x, g, w, *, tm, tn=512):
    rows = x.shape[0]
    return pl.pallas_call(
        functools.partial(_q_body, tn=tn),
        grid=(rows // tm,),
        in_specs=[
            pl.BlockSpec((tm, D_MODEL), lambda i: (i, 0)),
            _resident((1, D_MODEL)),
            _resident((D_MODEL, D_MODEL)),
        ],
        out_specs=pl.BlockSpec((tm, D_MODEL), lambda i: (i, 0)),
        out_shape=jax.ShapeDtypeStruct((rows, D_MODEL), BF16),
        compiler_params=_params(1),
        name="q_proj",
    )(x, g.reshape(1, D_MODEL), w)


def _attn_body(q_ref, kp_ref, kc_ref, km_ref, vp_ref, vc_ref, vm_ref, bias_ref, wq_ref, ones_ref,
               o_ref, *, cb, off):
    c0 = pl.program_id(1) * cb
    pad = jnp.zeros((KEY_PAD, KV_DIM), BF16)
    k_all = jnp.concatenate([kp_ref[0], kc_ref[0]], axis=0).astype(BF16)
    v_all = jnp.concatenate([vp_ref[0], vc_ref[0]], axis=0).astype(BF16)
    k_tail = jnp.concatenate([km_ref[0].astype(BF16), pad], axis=0)
    v_tail = jnp.concatenate([vm_ref[0].astype(BF16), pad], axis=0)
    lane_head = lax.broadcasted_iota(jnp.int32, (1, KV_DIM), 1) // HEAD_DIM
    zero = jnp.zeros((CHUNK, KV_DIM), BF16)
    for i in range(cb):
        rows = slice(i * CHUNK, (i + 1) * CHUNK)
        kc = jnp.concatenate([k_all[i * CHUNK:i * CHUNK + BAND], k_tail], axis=0)
        vc = jnp.concatenate([v_all[i * CHUNK:i * CHUNK + BAND], v_tail], axis=0)
        qn = []
        for g in range(GROUP):
            qg = q_ref[0, rows, g * KV_DIM:(g + 1) * KV_DIM].astype(F32)
            ss = jnp.dot((qg * qg).astype(BF16), ones_ref[...], preferred_element_type=F32)
            qn.append((qg * lax.rsqrt(ss * (1.0 / HEAD_DIM) + EPS) * wq_ref[...]).astype(BF16))
        qm = jnp.concatenate([jnp.where(lane_head == j, qn[g], zero)
                              for j in range(KV_HEADS) for g in range(GROUP)], axis=0)
        logits = lax.dot_general(qm, kc, (((1,), (1,)), ((), ())), preferred_element_type=F32)
        logits = logits + bias_ref[jnp.minimum(c0 + i + off, W_CHUNKS)]
        p = jnp.exp2(logits - jnp.max(logits, axis=-1, keepdims=True))
        denom = jnp.sum(p, axis=-1, keepdims=True)
        o = jnp.dot(p.astype(BF16), vc, preferred_element_type=F32) * (1.0 / denom)
        for g in range(GROUP):
            og = jnp.zeros((CHUNK, KV_DIM), F32)
            for j in range(KV_HEADS):
                r0 = (j * GROUP + g) * CHUNK
                og = jnp.where(lane_head == j, o[r0:r0 + CHUNK], og)
            o_ref[0, rows, g * KV_DIM:(g + 1) * KV_DIM] = og.astype(BF16)


def _attention(q, k, v, k_meta, v_meta, bias, wq_row, *, cb, chunk_offset):
    n, t, _ = q.shape
    off = chunk_offset
    assert (cb * CHUNK) % WIN_ROWS == 0 or cb == 1
    prev_spec = pl.BlockSpec(
        (1, WIN_ROWS, KV_DIM),
        lambda i, s: (i, jnp.maximum((s * cb + off) * CHUNK // WIN_ROWS - 1, 0), 0))
    cur_spec = pl.BlockSpec((1, cb * CHUNK, KV_DIM), lambda i, s: (i, s + off // cb, 0))
    meta_spec = pl.BlockSpec((1, N_META, KV_DIM), lambda i, s: (i, 0, 0))
    q_spec = pl.BlockSpec((1, cb * CHUNK, D_MODEL), lambda i, s: (i, s, 0))
    head_ones = (jnp.arange(KV_DIM)[:, None] // HEAD_DIM
                 == jnp.arange(KV_DIM)[None, :] // HEAD_DIM).astype(BF16)
    return pl.pallas_call(
        functools.partial(_attn_body, cb=cb, off=off),
        grid=(n, t // (cb * CHUNK)),
        in_specs=[
            q_spec, prev_spec, cur_spec, meta_spec, prev_spec, cur_spec, meta_spec,
            _resident((W_CHUNKS + 1, N_HEADS * CHUNK, KEY_COLS)),
            _resident((1, KV_DIM)),
            _resident((KV_DIM, KV_DIM)),
        ],
        out_specs=q_spec,
        out_shape=jax.ShapeDtypeStruct((n, t, D_MODEL), BF16),
        compiler_params=_params(2),
        name="window_attention",
    )(q, k, k, k_meta, v, v, v_meta, bias, wq_row, head_ones)


def _t5_bucket(rel):
    nb = N_BUCKETS // 2
    max_exact = nb // 2
    n = jnp.abs(rel)
    large = max_exact + (jnp.log(jnp.maximum(n, 1).astype(jnp.float32) / max_exact)
                         / math.log(MAX_DISTANCE / max_exact) * (nb - max_exact)).astype(jnp.int32)
    large = jnp.minimum(large, nb - 1)
    return jnp.where(rel > 0, nb, 0) + jnp.where(n < max_exact, n, large)


def _stack_heads(b):
    return jnp.moveaxis(b.astype(F32), -1, 0).reshape(N_HEADS * CHUNK, b.shape[1])


def _bias_tables(table, sink):
    i = jnp.arange(CHUNK)
    s = jnp.arange(BAND)
    m = jnp.arange(N_META)
    band = _stack_heads(table[_t5_bucket(s[None, :] - W_CHUNKS * CHUNK - i[:, None])])
    sink_col = jnp.repeat(sink.astype(F32), CHUNK)[:, None]
    pad = jnp.full((N_HEADS * CHUNK, KEY_PAD - 1), NEG_INF, F32)
    out = []
    for c in range(W_CHUNKS + 1):
        q_pos = N_META + c * CHUNK + i
        meta = _stack_heads(table[_t5_bucket(m[None, :] - q_pos[:, None])])
        valid = (c - W_CHUNKS + s // CHUNK >= 0)[None, :]
        out.append(jnp.concatenate([jnp.where(valid, band, NEG_INF), meta, sink_col, pad], axis=1))
    return jnp.stack(out) * LOG2E


def _perm_heads_cols(w):
    r = w.shape[0]
    return jnp.transpose(w.reshape(r, KV_HEADS, GROUP, HEAD_DIM), (0, 2, 1, 3)).reshape(r, D_MODEL)


def kernel(x_prompt, x_sample, state_conv, cache_meta_k, cache_meta_v, cache_win_k, cache_win_v,
           meta_tokens, ffn_norm, ffn_w_gate, ffn_w_up, ffn_w_down, conv_norm, conv_w_pw1,
           conv_b_pw1, conv_w_dw, conv_b_dw, conv_ln_g, conv_ln_b, conv_w_pw2, conv_b_pw2,
           kv_norm, w_k, w_v, k_norm, attn_norm, w_q, q_norm, sinks, w_o, rel_bias_table):
    n_p, seq, _ = x_prompt.shape
    n_s, s_new, _ = x_sample.shape
    rows_s = n_s * s_new
    pad_rows = CHUNK - N_META
    tm_ffn = 1024
    tm_big = 512
    tm_conv = 512
    tm_small = rows_s + N_META + pad_rows
    n_small = tm_small // s_new

    wgu, wd = _ffn_weights(ffn_w_gate, ffn_w_up, ffn_w_down)
    w_pw1 = conv_w_pw1[0].astype(BF16)
    w_pw2 = conv_w_pw2[0].astype(BF16)
    w_kv = jnp.concatenate([w_k, w_v], axis=1).astype(BF16)
    w_qp = _perm_heads_cols(w_q[0]).astype(BF16)
    w_op = _perm_heads_cols(w_o[0].T).T.astype(BF16)

    xb = x_prompt.reshape(n_p * seq, D_MODEL)
    xs = jnp.concatenate([x_sample.reshape(rows_s, D_MODEL), meta_tokens.astype(F32),
                          jnp.zeros((pad_rows, D_MODEL), F32)], axis=0)

    def ffn(x, layer, half, tm):
        return _ffn(x, ffn_norm, wgu, wd, layer, half, tm=tm)

    xs = ffn(xs, 0, 0, tm_small)
    xb = ffn(xb, 0, 0, tm_ffn)
    us = _pw1(xs, conv_norm[0], w_pw1, conv_b_pw1[0], tm=tm_small)
    us3 = us.reshape(n_small, s_new, D_MODEL)
    zpad = jnp.zeros((n_s, HALO_PAD, D_MODEL), F32)
    hist_s = jnp.concatenate([
        jnp.concatenate([zpad, state_conv[0].astype(F32)], axis=1),
        jnp.zeros((n_small - n_s, HALO, D_MODEL), F32)], axis=0)
    u_meta = us[rows_s:rows_s + N_META]
    hist_b = jnp.concatenate(
        [jnp.zeros((tm_conv // SUBLANES - N_META, D_MODEL), F32), u_meta], axis=0)[None]

    conv_args = (conv_w_dw[0], conv_b_dw[0], conv_ln_g[0], conv_ln_b[0])
    zs = _dwconv(us3, hist_s, *conv_args, tm=s_new).reshape(tm_small, D_MODEL)
    zb, u_tail = _conv_front(xb.reshape(n_p, seq, D_MODEL), hist_b, conv_norm[0], w_pw1,
                             conv_b_pw1[0], *conv_args, tm=tm_conv)
    zb = zb.reshape(n_p * seq, D_MODEL)
    xs = _mm_res(zs, w_pw2, conv_b_pw2[0], xs, tm=tm_small)
    xb = _mm_res(zb, w_pw2, conv_b_pw2[0], xb, tm=tm_big)
    xs = ffn(xs, 0, 1, tm_small)
    xb = ffn(xb, 0, 1, tm_ffn)

    ks, vs = _kv(xs, kv_norm, w_kv, k_norm, tm=tm_small)
    kb, vb = _kv(xb, kv_norm, w_kv, k_norm, tm=tm_big)
    k_new, v_new = ks[:rows_s], vs[:rows_s]
    k_lead, v_lead = ks[rows_s:rows_s + N_META], vs[rows_s:rows_s + N_META]
    xs = xs[:rows_s]

    xs = ffn(xs, 1, 0, rows_s)
    xb = ffn(xb, 1, 0, tm_ffn)
    qs = _q_proj(xs, attn_norm[0], w_qp, tm=rows_s).reshape(n_s, s_new, D_MODEL)
    qb = _q_proj(xb, attn_norm[0], w_qp, tm=tm_big).reshape(n_p, seq, D_MODEL)

    bias = _bias_tables(rel_bias_table, sinks[0])
    wq_row = (jnp.tile(q_norm[0].astype(F32), KV_HEADS) * (SCALE * LOG2E)).reshape(1, KV_DIM)

    flat_kv = lambda a: a.reshape(a.shape[0], a.shape[1], KV_DIM).astype(F32)
    k_cat = jnp.concatenate([flat_kv(cache_win_k), k_new.reshape(n_s, s_new, KV_DIM)], axis=1)
    v_cat = jnp.concatenate([flat_kv(cache_win_v), v_new.reshape(n_s, s_new, KV_DIM)], axis=1)
    os_ = _attention(qs, k_cat, v_cat, flat_kv(cache_meta_k), flat_kv(cache_meta_v),
                     bias, wq_row, cb=1, chunk_offset=W_CHUNKS)
    lead = lambda a: jnp.broadcast_to(a[None], (n_p, N_META, KV_DIM))
    ob = _attention(qb, kb.reshape(n_p, seq, KV_DIM), vb.reshape(n_p, seq, KV_DIM),
                    lead(k_lead), lead(v_lead), bias, wq_row, cb=8, chunk_offset=0)

    xs = _mm_res(os_.reshape(rows_s, D_MODEL), w_op, None, xs, tm=rows_s)
    xb = _mm_res(ob.reshape(n_p * seq, D_MODEL), w_op, None, xb, tm=tm_big)
    xs = ffn(xs, 1, 1, rows_s)
    xb = ffn(xb, 1, 1, tm_ffn)

    heads = lambda a, n: a.reshape(n, -1, KV_HEADS, HEAD_DIM)
    keep = CONV_W - 1
    return (
        xb.reshape(n_p, seq, D_MODEL),
        xs.reshape(n_s, s_new, D_MODEL),
        u_tail[:, HALO - keep:][None],
        heads(lead(k_lead), n_p),
        heads(lead(v_lead), n_p),
        heads(kb.reshape(n_p, seq, KV_DIM)[:, seq - WIN_ROWS:], n_p),
        heads(vb.reshape(n_p, seq, KV_DIM)[:, seq - WIN_ROWS:], n_p),
        us3[:n_s, s_new - keep:][None],
        heads(k_new, n_s),
        heads(v_new, n_s),
    )
```

```python
import functools
import math

import jax
import jax.numpy as jnp
from jax import lax
from jax.experimental import pallas as pl
from jax.experimental.pallas import tpu as pltpu

D_MODEL = 2048
D_FF = 5632
CHUNK = 64
N_META = 16
CONV_W = 31
N_HEADS = 32
HEAD_DIM = 64
KV_HEADS = 4
GROUP = N_HEADS // KV_HEADS
KV_DIM = KV_HEADS * HEAD_DIM
WINDOW = 128
W_CHUNKS = -(-WINDOW // CHUNK)
WIN_ROWS = W_CHUNKS * CHUNK
BAND = (W_CHUNKS + 1) * CHUNK
N_BUCKETS = 32
MAX_DISTANCE = 128
EPS = 1e-6
SCALE = HEAD_DIM ** -0.5
NEG_INF = -1e30
LOG2E = math.log2(math.e)

LANES = 128
SUBLANES = 8
N_SLABS = D_MODEL // LANES
HALO = 32
HALO_PAD = HALO - (CONV_W - 1)
SEG_PAD = 4
KEY_COLS = 256
KEY_PAD = KEY_COLS - BAND - N_META
V7X_VMEM_LIMIT = 56 * 1024 * 1024

F32 = jnp.float32
BF16 = jnp.bfloat16


def _params(n_axes, vmem=V7X_VMEM_LIMIT):
    return pltpu.CompilerParams(
        dimension_semantics=("arbitrary",) * n_axes, vmem_limit_bytes=vmem)


def _resident(shape):
    return pl.BlockSpec(shape, lambda *_: (0,) * len(shape), pipeline_mode=pl.Buffered(1))


def _rms_bf16(x, g):
    ms = jnp.mean(x * x, axis=-1, keepdims=True)
    return (x * lax.rsqrt(ms + EPS) * g).astype(BF16)


def _ffn_init(x_ref, g_ref, o_ref, h_ref):
    x = x_ref[...]
    h_ref[...] = _rms_bf16(x, g_ref[...])
    o_ref[...] = x


def _ffn_step(h_ref, wg, wu, wd, o_ref):
    h = h_ref[...]
    gate = jnp.dot(h, wg, preferred_element_type=F32)
    up = jnp.dot(h, wu, preferred_element_type=F32)
    act = (0.5 * gate * jax.nn.sigmoid(gate) * up).astype(BF16)
    o_ref[...] += jnp.dot(act, wd, preferred_element_type=F32)


def _ffn_body(x_ref, g_ref, wg_ref, wu_ref, wd_ref, o_ref, h_ref):
    pl.when(pl.program_id(1) == 0)(lambda: _ffn_init(x_ref, g_ref, o_ref, h_ref))
    _ffn_step(h_ref, wg_ref[...], wu_ref[...], wd_ref[...], o_ref)


def _ffn_cast_body(x_ref, g_ref, wg_ref, wu_ref, wd_ref, o_ref, wg_out, wu_out, wd_out, h_ref):
    pl.when(pl.program_id(0) == 0)(lambda: _ffn_init(x_ref, g_ref, o_ref, h_ref))
    wg_out[...] = wg_ref[...].astype(BF16)
    wu_out[...] = wu_ref[...].astype(BF16)
    wd_out[...] = wd_ref[...].astype(BF16)
    _ffn_step(h_ref, wg_out[...], wu_out[...], wd_out[...], o_ref)


def _ffn_gain_spec(layer, half):
    return pl.BlockSpec((None, 1, D_MODEL), lambda *_: (layer * 2 + half, 0, 0))


def _ffn(x, g, wg, wu, wd, layer, half, *, tm, tf=512):
    rows = x.shape[0]
    return pl.pallas_call(
        _ffn_body,
        grid=(rows // tm, D_FF // tf),
        in_specs=[
            pl.BlockSpec((tm, D_MODEL), lambda i, j: (i, 0)),
            _ffn_gain_spec(layer, half),
            pl.BlockSpec((D_MODEL, tf), lambda i, j: (0, j)),
            pl.BlockSpec((D_MODEL, tf), lambda i, j: (0, j)),
            pl.BlockSpec((tf, D_MODEL), lambda i, j: (j, 0)),
        ],
        out_specs=pl.BlockSpec((tm, D_MODEL), lambda i, j: (i, 0)),
        out_shape=jax.ShapeDtypeStruct((rows, D_MODEL), F32),
        scratch_shapes=[pltpu.VMEM((tm, D_MODEL), BF16)],
        compiler_params=_params(2),
        name="ffn",
    )(x, g.reshape(-1, 1, D_MODEL), wg, wu, wd)


def _ffn_cast(x, g, w_gate, w_up, w_down, layer, half, *, tf=256):
    rows = x.shape[0]
    f32_cols = pl.BlockSpec((None, None, D_MODEL, tf), lambda j: (layer, half, 0, j))
    bf16_cols = pl.BlockSpec((D_MODEL, tf), lambda j: (0, j))
    cols_shape = jax.ShapeDtypeStruct((D_MODEL, D_FF), BF16)
    return pl.pallas_call(
        _ffn_cast_body,
        grid=(D_FF // tf,),
        in_specs=[
            _resident((rows, D_MODEL)),
            _ffn_gain_spec(layer, half),
            f32_cols, f32_cols,
            pl.BlockSpec((None, None, tf, D_MODEL), lambda j: (layer, half, j, 0)),
        ],
        out_specs=[pl.BlockSpec((rows, D_MODEL), lambda j: (0, 0)), bf16_cols, bf16_cols,
                   pl.BlockSpec((tf, D_MODEL), lambda j: (j, 0))],
        out_shape=[jax.ShapeDtypeStruct((rows, D_MODEL), F32), cols_shape, cols_shape,
                   jax.ShapeDtypeStruct((D_FF, D_MODEL), BF16)],
        scratch_shapes=[pltpu.VMEM((rows, D_MODEL), BF16)],
        compiler_params=_params(1),
        name="ffn_cast",
    )(x, g.reshape(-1, 1, D_MODEL), w_gate, w_up, w_down)


def _pw1_body(x_ref, g_ref, w_ref, b_ref, u_ref, *, tn):
    h = _rms_bf16(x_ref[...], g_ref[...])
    for c0 in range(0, D_MODEL, tn):
        a = jnp.dot(h, w_ref[:, c0:c0 + tn], preferred_element_type=F32) + b_ref[:, c0:c0 + tn]
        gate = (jnp.dot(h, w_ref[:, D_MODEL + c0:D_MODEL + c0 + tn], preferred_element_type=F32)
                + b_ref[:, D_MODEL + c0:D_MODEL + c0 + tn])
        u_ref[:, c0:c0 + tn] = a * jax.nn.sigmoid(gate)


def _pw1(x, g, w, b, *, tm, tn=512):
    rows = x.shape[0]
    return pl.pallas_call(
        functools.partial(_pw1_body, tn=tn),
        grid=(rows // tm,),
        in_specs=[
            pl.BlockSpec((tm, D_MODEL), lambda i: (i, 0)),
            _resident((1, D_MODEL)),
            _resident((D_MODEL, 2 * D_MODEL)),
            _resident((1, 2 * D_MODEL)),
        ],
        out_specs=pl.BlockSpec((tm, D_MODEL), lambda i: (i, 0)),
        out_shape=jax.ShapeDtypeStruct((rows, D_MODEL), F32),
        compiler_params=_params(1),
        name="pw1_glu",
    )(x, g.reshape(1, D_MODEL), w, b.reshape(1, 2 * D_MODEL))


def _conv_taps(win_ref, w_ref, b_ref, y_ref, c, *, seg, hist_rows, q_group=SUBLANES):
    pitch = seg + SEG_PAD
    q_group = min(seg, q_group)
    for q0 in range(0, seg, q_group):
        accs = [b_ref[c]] * q_group
        for k in range(CONV_W):
            wv = w_ref[c, k]
            for qi in range(q_group):
                rho = q0 + qi + hist_rows - (CONV_W - 1) + k
                start = rho + SEG_PAD * (rho // seg)
                accs[qi] = accs[qi] + wv * win_ref[c, pl.ds(start, SUBLANES, stride=pitch), :]
        for qi in range(q_group):
            y_ref[c, pl.ds(q0 + qi, SUBLANES, stride=seg), :] = accs[qi]


def _ln_silu_store(y_ref, lg_ref, lb_ref, z_ref, *, tm):
    tot = y_ref[0]
    for c in range(1, N_SLABS):
        tot = tot + y_ref[c]
    mu = jnp.sum(tot, axis=-1, keepdims=True) * (1.0 / D_MODEL)
    sq = jnp.zeros((tm, LANES), F32)
    for c in range(N_SLABS):
        d = y_ref[c] - mu
        sq = sq + d * d
    inv = lax.rsqrt(jnp.sum(sq, axis=-1, keepdims=True) * (1.0 / D_MODEL) + EPS)
    for c in range(N_SLABS):
        lanes = slice(c * LANES, (c + 1) * LANES)
        ln = (y_ref[c] - mu) * inv * lg_ref[:, lanes] + lb_ref[:, lanes]
        z_ref[0, :, lanes] = (ln * jax.nn.sigmoid(ln)).astype(BF16)


def _dwconv_body(u_ref, halo_ref, hist_ref, w_ref, b_ref, lg_ref, lb_ref, z_ref,
                 win_ref, y_ref, *, tm):
    seg = tm // SUBLANES
    pitch = seg + SEG_PAD
    halo = jnp.where(pl.program_id(1) == 0, hist_ref[0], halo_ref[0])
    for c in range(N_SLABS):
        lanes = slice(c * LANES, (c + 1) * LANES)
        for sg in range((HALO + tm) // seg):
            r0 = sg * seg
            src = halo[r0:r0 + seg, lanes] if r0 < HALO else u_ref[0, r0 - HALO:r0 - HALO + seg, lanes]
            win_ref[c, sg * pitch:sg * pitch + seg, :] = src

    def slab(c, carry):
        _conv_taps(win_ref, w_ref, b_ref, y_ref, c, seg=seg, hist_rows=HALO)
        return carry

    lax.fori_loop(0, N_SLABS, slab, 0)
    _ln_silu_store(y_ref, lg_ref, lb_ref, z_ref, tm=tm)


def _conv_tables(w_dw, b_dw):
    w_b = jnp.broadcast_to(
        jnp.transpose(w_dw.astype(F32).reshape(CONV_W, N_SLABS, LANES), (1, 0, 2))[:, :, None, :],
        (N_SLABS, CONV_W, SUBLANES, LANES))
    b_b = jnp.broadcast_to(b_dw.astype(F32).reshape(N_SLABS, 1, LANES), (N_SLABS, SUBLANES, LANES))
    return w_b, b_b


def _win_rows(hist_rows, tm):
    seg = tm // SUBLANES
    return -(-((hist_rows + tm) // seg * (seg + SEG_PAD)) // SUBLANES) * SUBLANES


def _dwconv(u, hist, w_dw, b_dw, ln_g, ln_b, *, tm):
    n, t, _ = u.shape
    per_stream = hist.shape[0] == n
    hb = tm // HALO
    win_rows = _win_rows(HALO, tm)
    row = lambda a: a.reshape(1, D_MODEL)
    w_b, b_b = _conv_tables(w_dw, b_dw)
    return pl.pallas_call(
        functools.partial(_dwconv_body, tm=tm),
        grid=(n, t // tm),
        in_specs=[
            pl.BlockSpec((1, tm, D_MODEL), lambda i, j: (i, j, 0)),
            pl.BlockSpec((1, HALO, D_MODEL), lambda i, j: (i, jnp.maximum(j * hb - 1, 0), 0)),
            pl.BlockSpec((1, HALO, D_MODEL), (lambda i, j: (i, 0, 0)) if per_stream
                         else (lambda i, j: (0, 0, 0))),
            _resident((N_SLABS, CONV_W, SUBLANES, LANES)),
            _resident((N_SLABS, SUBLANES, LANES)),
            _resident((1, D_MODEL)),
            _resident((1, D_MODEL)),
        ],
        out_specs=pl.BlockSpec((1, tm, D_MODEL), lambda i, j: (i, j, 0)),
        out_shape=jax.ShapeDtypeStruct((n, t, D_MODEL), BF16),
        scratch_shapes=[pltpu.VMEM((N_SLABS, win_rows, LANES), F32),
                        pltpu.VMEM((N_SLABS, tm, LANES), F32)],
        compiler_params=_params(2),
        name="dwconv_ln_silu",
    )(u, u, hist, w_b, b_b, row(ln_g), row(ln_b))


def _mm_res_body(z_ref, w_ref, b_ref, x_ref, o_ref, *, tn):
    z = z_ref[...]
    for c0 in range(0, D_MODEL, tn):
        cols = slice(c0, c0 + tn)
        o_ref[:, cols] = (x_ref[:, cols] + jnp.dot(z, w_ref[:, cols], preferred_element_type=F32)
                          + b_ref[:, cols])


def _mm_res_nobias_body(z_ref, w_ref, x_ref, o_ref, *, tn):
    z = z_ref[...]
    for c0 in range(0, D_MODEL, tn):
        cols = slice(c0, c0 + tn)
        o_ref[:, cols] = x_ref[:, cols] + jnp.dot(z, w_ref[:, cols], preferred_element_type=F32)


def _mm_res(z, w, b, x, *, tm, tn=512):
    rows = x.shape[0]
    row_spec = pl.BlockSpec((tm, D_MODEL), lambda i: (i, 0))
    w_spec = _resident((D_MODEL, D_MODEL))
    if b is None:
        body, in_specs, args = _mm_res_nobias_body, [row_spec, w_spec, row_spec], (z, w, x)
    else:
        body = _mm_res_body
        in_specs = [row_spec, w_spec, _resident((1, D_MODEL)), row_spec]
        args = (z, w, b.reshape(1, D_MODEL), x)
    return pl.pallas_call(
        functools.partial(body, tn=tn),
        grid=(rows // tm,),
        in_specs=in_specs,
        out_specs=row_spec,
        out_shape=jax.ShapeDtypeStruct((rows, D_MODEL), F32),
        compiler_params=_params(1),
        name="matmul_residual",
    )(*args)


def _kv_body(x_ref, g_ref, w_ref, kn_ref, k_ref, v_ref):
    h = _rms_bf16(x_ref[...], g_ref[...])
    kv = jnp.dot(h, w_ref[...], preferred_element_type=F32)
    k = kv[:, :KV_DIM]
    lane_head = lax.broadcasted_iota(jnp.int32, (1, KV_DIM), 1) // HEAD_DIM
    inv = jnp.zeros_like(k)
    for j in range(KV_HEADS):
        sel = lane_head == j
        ss = jnp.sum(jnp.where(sel, k * k, 0.0), axis=-1, keepdims=True)
        inv = jnp.where(sel, lax.rsqrt(ss * (1.0 / HEAD_DIM) + EPS), inv)
    k_ref[...] = k * inv * kn_ref[...]
    v_ref[...] = kv[:, KV_DIM:]


def _kv(x, g, w_kv, k_norm, *, tm):
    rows = x.shape[0]
    out = jax.ShapeDtypeStruct((rows, KV_DIM), F32)
    return pl.pallas_call(
        _kv_body,
        grid=(rows // tm,),
        in_specs=[
            pl.BlockSpec((tm, D_MODEL), lambda i: (i, 0)),
            _resident((1, D_MODEL)),
            _resident((D_MODEL, 2 * KV_DIM)),
            _resident((1, KV_DIM)),
        ],
        out_specs=[pl.BlockSpec((tm, KV_DIM), lambda i: (i, 0))] * 2,
        out_shape=[out, out],
        compiler_params=_params(1),
        name="shared_kv",
    )(x, g.reshape(1, D_MODEL), w_kv, jnp.tile(k_norm, KV_HEADS).reshape(1, KV_DIM))


def _q_body(x_ref, g_ref, w_ref, q_ref, *, tn):
    h = _rms_bf16(x_ref[...], g_ref[...])
    for c0 in range(0, D_MODEL, tn):
        cols = slice(c0, c0 + tn)
        q_ref[:, cols] = jnp.dot(h, w_ref[:, cols], preferred_element_type=F32).astype(BF16)


def _q_proj(x, g, w, *, tm, tn=512):
    rows = x.shape[0]
    return pl.pallas_call(
        functools.partial(_q_body, tn=tn),
        grid=(rows // tm,),
        in_specs=[
            pl.BlockSpec((tm, D_MODEL), lambda i: (i, 0)),
            _resident((1, D_MODEL)),
            _resident((D_MODEL, D_MODEL)),
        ],
        out_specs=pl.BlockSpec((tm, D_MODEL), lambda i: (i, 0)),
        out_shape=jax.ShapeDtypeStruct((rows, D_MODEL), BF16),
        compiler_params=_params(1),
        name="q_proj",
    )(x, g.reshape(1, D_MODEL), w)


def _attn_body(q_ref, kp_ref, kc_ref, km_ref, vp_ref, vc_ref, vm_ref, bias_ref, wq_ref, ones_ref,
               o_ref, *, cb, off):
    c0 = pl.program_id(1) * cb
    pad = jnp.zeros((KEY_PAD, KV_DIM), BF16)
    k_all = jnp.concatenate([kp_ref[0], kc_ref[0]], axis=0).astype(BF16)
    v_all = jnp.concatenate([vp_ref[0], vc_ref[0]], axis=0).astype(BF16)
    k_tail = jnp.concatenate([km_ref[0].astype(BF16), pad], axis=0)
    v_tail = jnp.concatenate([vm_ref[0].astype(BF16), pad], axis=0)
    lane_head = lax.broadcasted_iota(jnp.int32, (1, KV_DIM), 1) // HEAD_DIM
    zero = jnp.zeros((CHUNK, KV_DIM), BF16)
    for i in range(cb):
        rows = slice(i * CHUNK, (i + 1) * CHUNK)
        kc = jnp.concatenate([k_all[i * CHUNK:i * CHUNK + BAND], k_tail], axis=0)
        vc = jnp.concatenate([v_all[i * CHUNK:i * CHUNK + BAND], v_tail], axis=0)
        qn = []
        for g in range(GROUP):
            qg = q_ref[0, rows, g * KV_DIM:(g + 1) * KV_DIM].astype(F32)
            ss = jnp.dot((qg * qg).astype(BF16), ones_ref[...], preferred_element_type=F32)
            qn.append((qg * lax.rsqrt(ss * (1.0 / HEAD_DIM) + EPS) * wq_ref[...]).astype(BF16))
        qm = jnp.concatenate([jnp.where(lane_head == j, qn[g], zero)
                              for j in range(KV_HEADS) for g in range(GROUP)], axis=0)
        logits = lax.dot_general(qm, kc, (((1,), (1,)), ((), ())), preferred_element_type=F32)
        logits = logits + bias_ref[jnp.minimum(c0 + i + off, W_CHUNKS)]
        p = jnp.exp2(logits - jnp.max(logits, axis=-1, keepdims=True))
        denom = jnp.sum(p, axis=-1, keepdims=True)
        o = jnp.dot(p.astype(BF16), vc, preferred_element_type=F32) * (1.0 / denom)
        for g in range(GROUP):
            og = jnp.zeros((CHUNK, KV_DIM), F32)
            for j in range(KV_HEADS):
                r0 = (j * GROUP + g) * CHUNK
                og = jnp.where(lane_head == j, o[r0:r0 + CHUNK], og)
            o_ref[0, rows, g * KV_DIM:(g + 1) * KV_DIM] = og.astype(BF16)


def _attention(q, k, v, k_meta, v_meta, bias, wq_row, *, cb, chunk_offset):
    n, t, _ = q.shape
    off = chunk_offset
    assert (cb * CHUNK) % WIN_ROWS == 0 or cb == 1
    prev_spec = pl.BlockSpec(
        (1, WIN_ROWS, KV_DIM),
        lambda i, s: (i, jnp.maximum((s * cb + off) * CHUNK // WIN_ROWS - 1, 0), 0))
    cur_spec = pl.BlockSpec((1, cb * CHUNK, KV_DIM), lambda i, s: (i, s + off // cb, 0))
    meta_spec = pl.BlockSpec((1, N_META, KV_DIM), lambda i, s: (i, 0, 0))
    q_spec = pl.BlockSpec((1, cb * CHUNK, D_MODEL), lambda i, s: (i, s, 0))
    head_ones = (jnp.arange(KV_DIM)[:, None] // HEAD_DIM
                 == jnp.arange(KV_DIM)[None, :] // HEAD_DIM).astype(BF16)
    return pl.pallas_call(
        functools.partial(_attn_body, cb=cb, off=off),
        grid=(n, t // (cb * CHUNK)),
        in_specs=[
            q_spec, prev_spec, cur_spec, meta_spec, prev_spec, cur_spec, meta_spec,
            _resident((W_CHUNKS + 1, N_HEADS * CHUNK, KEY_COLS)),
            _resident((1, KV_DIM)),
            _resident((KV_DIM, KV_DIM)),
        ],
        out_specs=q_spec,
        out_shape=jax.ShapeDtypeStruct((n, t, D_MODEL), BF16),
        compiler_params=_params(2),
        name="window_attention",
    )(q, k, k, k_meta, v, v, v_meta, bias, wq_row, head_ones)


def _t5_bucket(rel):
    nb = N_BUCKETS // 2
    max_exact = nb // 2
    n = jnp.abs(rel)
    large = max_exact + (jnp.log(jnp.maximum(n, 1).astype(jnp.float32) / max_exact)
                         / math.log(MAX_DISTANCE / max_exact) * (nb - max_exact)).astype(jnp.int32)
    large = jnp.minimum(large, nb - 1)
    return jnp.where(rel > 0, nb, 0) + jnp.where(n < max_exact, n, large)


def _stack_heads(b):
    return jnp.moveaxis(b.astype(F32), -1, 0).reshape(N_HEADS * CHUNK, b.shape[1])


def _bias_tables(table, sink):
    i = jnp.arange(CHUNK)
    s = jnp.arange(BAND)
    m = jnp.arange(N_META)
    band = _stack_heads(table[_t5_bucket(s[None, :] - W_CHUNKS * CHUNK - i[:, None])])
    sink_col = jnp.repeat(sink.astype(F32), CHUNK)[:, None]
    pad = jnp.full((N_HEADS * CHUNK, KEY_PAD - 1), NEG_INF, F32)
    out = []
    for c in range(W_CHUNKS + 1):
        q_pos = N_META + c * CHUNK + i
        meta = _stack_heads(table[_t5_bucket(m[None, :] - q_pos[:, None])])
        valid = (c - W_CHUNKS + s // CHUNK >= 0)[None, :]
        out.append(jnp.concatenate([jnp.where(valid, band, NEG_INF), meta, sink_col, pad], axis=1))
    return jnp.stack(out) * LOG2E


def _perm_heads_cols(w):
    r = w.shape[0]
    return jnp.transpose(w.reshape(r, KV_HEADS, GROUP, HEAD_DIM), (0, 2, 1, 3)).reshape(r, D_MODEL)


def kernel(x_prompt, x_sample, state_conv, cache_meta_k, cache_meta_v, cache_win_k, cache_win_v,
           meta_tokens, ffn_norm, ffn_w_gate, ffn_w_up, ffn_w_down, conv_norm, conv_w_pw1,
           conv_b_pw1, conv_w_dw, conv_b_dw, conv_ln_g, conv_ln_b, conv_w_pw2, conv_b_pw2,
           kv_norm, w_k, w_v, k_norm, attn_norm, w_q, q_norm, sinks, w_o, rel_bias_table):
    n_p, seq, _ = x_prompt.shape
    n_s, s_new, _ = x_sample.shape
    rows_s = n_s * s_new
    pad_rows = CHUNK - N_META
    tm_ffn = 1024
    tm_big = 512
    tm_conv = 256
    tm_small = rows_s + N_META + pad_rows
    n_small = tm_small // s_new

    w_pw1 = conv_w_pw1[0].astype(BF16)
    w_pw2 = conv_w_pw2[0].astype(BF16)
    w_kv = jnp.concatenate([w_k, w_v], axis=1).astype(BF16)
    w_qp = _perm_heads_cols(w_q[0]).astype(BF16)
    w_op = _perm_heads_cols(w_o[0].T).T.astype(BF16)

    xb = x_prompt.reshape(n_p * seq, D_MODEL)
    xs = jnp.concatenate([x_sample.reshape(rows_s, D_MODEL), meta_tokens.astype(F32),
                          jnp.zeros((pad_rows, D_MODEL), F32)], axis=0)

    def ffn(xs, xb, layer, half):
        xs, wg, wu, wd = _ffn_cast(xs, ffn_norm, ffn_w_gate, ffn_w_up, ffn_w_down, layer, half)
        return xs, _ffn(xb, ffn_norm, wg, wu, wd, layer, half, tm=tm_ffn)

    xs, xb = ffn(xs, xb, 0, 0)
    us = _pw1(xs, conv_norm[0], w_pw1, conv_b_pw1[0], tm=tm_small)
    ub = _pw1(xb, conv_norm[0], w_pw1, conv_b_pw1[0], tm=tm_big)
    us3 = us.reshape(n_small, s_new, D_MODEL)
    ub3 = ub.reshape(n_p, seq, D_MODEL)
    zpad = jnp.zeros((n_s, HALO_PAD, D_MODEL), F32)
    hist_s = jnp.concatenate([
        jnp.concatenate([zpad, state_conv[0].astype(F32)], axis=1),
        jnp.zeros((n_small - n_s, HALO, D_MODEL), F32)], axis=0)
    u_meta = us[rows_s:rows_s + N_META]
    hist_b = jnp.concatenate([jnp.zeros((HALO - N_META, D_MODEL), F32), u_meta], axis=0)[None]

    conv_args = (conv_w_dw[0], conv_b_dw[0], conv_ln_g[0], conv_ln_b[0])
    zs = _dwconv(us3, hist_s, *conv_args, tm=s_new).reshape(tm_small, D_MODEL)
    zb = _dwconv(ub3, hist_b, *conv_args, tm=tm_conv).reshape(n_p * seq, D_MODEL)
    xs = _mm_res(zs, w_pw2, conv_b_pw2[0], xs, tm=tm_small)
    xb = _mm_res(zb, w_pw2, conv_b_pw2[0], xb, tm=tm_big)
    xs, xb = ffn(xs, xb, 0, 1)

    ks, vs = _kv(xs, kv_norm, w_kv, k_norm, tm=tm_small)
    kb, vb = _kv(xb, kv_norm, w_kv, k_norm, tm=tm_ffn)
    k_new, v_new = ks[:rows_s], vs[:rows_s]
    k_lead, v_lead = ks[rows_s:rows_s + N_META], vs[rows_s:rows_s + N_META]
    xs = xs[:rows_s]

    xs, xb = ffn(xs, xb, 1, 0)
    qs = _q_proj(xs, attn_norm[0], w_qp, tm=rows_s).reshape(n_s, s_new, D_MODEL)
    qb = _q_proj(xb, attn_norm[0], w_qp, tm=tm_ffn).reshape(n_p, seq, D_MODEL)

    bias = _bias_tables(rel_bias_table, sinks[0])
    wq_row = (jnp.tile(q_norm[0].astype(F32), KV_HEADS) * (SCALE * LOG2E)).reshape(1, KV_DIM)

    flat_kv = lambda a: a.reshape(a.shape[0], a.shape[1], KV_DIM).astype(F32)
    k_cat = jnp.concatenate([flat_kv(cache_win_k), k_new.reshape(n_s, s_new, KV_DIM)], axis=1)
    v_cat = jnp.concatenate([flat_kv(cache_win_v), v_new.reshape(n_s, s_new, KV_DIM)], axis=1)
    os_ = _attention(qs, k_cat, v_cat, flat_kv(cache_meta_k), flat_kv(cache_meta_v),
                     bias, wq_row, cb=1, chunk_offset=W_CHUNKS)
    lead = lambda a: jnp.broadcast_to(a[None], (n_p, N_META, KV_DIM))
    ob = _attention(qb, kb.reshape(n_p, seq, KV_DIM), vb.reshape(n_p, seq, KV_DIM),
                    lead(k_lead), lead(v_lead), bias, wq_row, cb=8, chunk_offset=0)

    xs = _mm_res(os_.reshape(rows_s, D_MODEL), w_op, None, xs, tm=rows_s)
    xb = _mm_res(ob.reshape(n_p * seq, D_MODEL), w_op, None, xb, tm=tm_big)
    xs, xb = ffn(xs, xb, 1, 1)

    heads = lambda a, n: a.reshape(n, -1, KV_HEADS, HEAD_DIM)
    keep = CONV_W - 1
    return (
        xb.reshape(n_p, seq, D_MODEL),
        xs.reshape(n_s, s_new, D_MODEL),
        ub3[:, seq - keep:][None],
        heads(lead(k_lead), n_p),
        heads(lead(v_lead), n_p),
        heads(kb.reshape(n_p, seq, KV_DIM)[:, seq - WIN_ROWS:], n_p),
        heads(vb.reshape(n_p, seq, KV_DIM)[:, seq - WIN_ROWS:], n_p),
        us3[:n_s, s_new - keep:][None],
        heads(k_new, n_s),
        heads(v_new, n_s),
    )
```
